```python
import math
import jax, jax.numpy as jnp
from jax import lax
import numpy as np

D_MODEL = 2048
BATCH = 1
SEQ = 16384
DEPTH = 1

CHUNK = 64
EPS = 1e-6

S5_WIDTH = D_MODEL // 2
S5_GROUP = 16
S5_GROUPS = S5_WIDTH // S5_GROUP
S5_STATE = 64
S5_DT_MIN = 1e-3
S5_DT_MAX = 1e-1

RET_HEADS = 8
RET_WIDTH = D_MODEL // 2
RET_V_HEAD = RET_WIDTH // RET_HEADS
RET_QK_HEAD = RET_V_HEAD // 2
RET_QK = RET_HEADS * RET_QK_HEAD
ROPE_BASE = 10000.0

COL_SIZES = (S5_WIDTH, S5_WIDTH, RET_QK, RET_QK, RET_WIDTH, RET_WIDTH, D_MODEL, D_MODEL)
IN_COLS = 2 * S5_WIDTH + 2 * RET_QK + 2 * RET_WIDTH + 2 * D_MODEL
SPLIT_IDX = (S5_WIDTH,
             2 * S5_WIDTH,
             2 * S5_WIDTH + RET_QK,
             2 * S5_WIDTH + 2 * RET_QK,
             2 * S5_WIDTH + 2 * RET_QK + RET_WIDTH,
             2 * S5_WIDTH + 2 * RET_QK + 2 * RET_WIDTH,
             2 * S5_WIDTH + 2 * RET_QK + 2 * RET_WIDTH + D_MODEL)

kernel_name = "hybrid_s5_retention_gated_block"


def rms_norm(x, gain):
    xf = x.astype(jnp.float32)
    y = xf * lax.rsqrt(jnp.mean(xf * xf, axis=-1, keepdims=True) + EPS)
    return (y * gain.astype(jnp.float32)).astype(x.dtype)


def s5_mixer(u, lam_re, lam_im, log_step, b_re, b_im, c_re, c_im, d_skip, w_glu, b_glu):
    bsz, seq, _ = u.shape
    f32 = jnp.float32
    uf = u.astype(f32).reshape(bsz, seq, S5_GROUPS, S5_GROUP)
    lr = lam_re.astype(f32)
    li = lam_im.astype(f32)
    step = jnp.exp(log_step.astype(f32))[:, None]
    mag = jnp.exp(lr * step)
    ang = li * step
    ab_re = mag * jnp.cos(ang)
    ab_im = mag * jnp.sin(ang)
    den = lr * lr + li * li
    nr = ab_re - 1.0
    f_re = (nr * lr + ab_im * li) / den
    f_im = (ab_im * lr - nr * li) / den
    br = b_re.astype(f32)
    bi = b_im.astype(f32)
    bb_re = f_re[..., None] * br - f_im[..., None] * bi
    bb_im = f_re[..., None] * bi + f_im[..., None] * br
    bu_re = jnp.einsum('blgh,gph->blgp', uf, bb_re)
    bu_im = jnp.einsum('blgh,gph->blgp', uf, bb_im)
    a_re = jnp.broadcast_to(ab_re, bu_re.shape)
    a_im = jnp.broadcast_to(ab_im, bu_re.shape)

    def combine(e1, e2):
        a1r, a1i, b1r, b1i = e1
        a2r, a2i, b2r, b2i = e2
        return (a2r * a1r - a2i * a1i,
                a2r * a1i + a2i * a1r,
                a2r * b1r - a2i * b1i + b2r,
                a2r * b1i + a2i * b1r + b2i)

    _, _, xr, xi = lax.associative_scan(combine, (a_re, a_im, bu_re, bu_im), axis=1)
    y = (jnp.einsum('blgp,ghp->blgh', xr, c_re.astype(f32))
         - jnp.einsum('blgp,ghp->blgh', xi, c_im.astype(f32))
         + d_skip.astype(f32).reshape(S5_GROUPS, S5_GROUP) * uf)
    y = y.reshape(bsz, seq, S5_WIDTH)
    g = jax.nn.gelu(y)
    out = g * jax.nn.sigmoid(g @ w_glu.astype(f32) + b_glu.astype(f32))
    return out.astype(u.dtype)


def rotary(t, seq):
    d = t.shape[-1]
    inv = ROPE_BASE ** (-jnp.arange(0, d, 2, dtype=jnp.float32) / d)
    pos = jnp.arange(seq, dtype=jnp.float32)
    angle = pos[:, None] * inv[None, :]
    cos = jnp.cos(angle)[None, :, None, :]
    sin = jnp.sin(angle)[None, :, None, :]
    t1, t2 = t[..., : d // 2], t[..., d // 2:]
    return jnp.concatenate([t1 * cos - t2 * sin, t1 * sin + t2 * cos], axis=-1)


def retention(q, k, v):
    bsz, seq, _ = q.shape
    f32 = jnp.float32
    n = seq // CHUNK
    qh = rotary(q.astype(f32).reshape(bsz, seq, RET_HEADS, RET_QK_HEAD), seq)
    kh = rotary(k.astype(f32).reshape(bsz, seq, RET_HEADS, RET_QK_HEAD), seq) * (RET_QK_HEAD ** -0.5)
    vh = v.astype(f32).reshape(bsz, seq, RET_HEADS, RET_V_HEAD)
    qc = qh.reshape(bsz, n, CHUNK, RET_HEADS, RET_QK_HEAD)
    kc = kh.reshape(bsz, n, CHUNK, RET_HEADS, RET_QK_HEAD)
    vc = vh.reshape(bsz, n, CHUNK, RET_HEADS, RET_V_HEAD)
    log_g = jnp.log1p(-jnp.exp2(-5.0 - jnp.arange(RET_HEADS, dtype=f32)))
    idx = jnp.arange(CHUNK, dtype=f32)
    dmat = jnp.exp(jnp.abs(idx[:, None] - idx[None, :])[None] * log_g[:, None, None])
    scores = jnp.einsum('bnihd,bnjhd->bnhij', qc, kc) * dmat
    inner = jnp.einsum('bnhij,bnjhe->bnihe', scores, vc)
    zeta = jnp.exp((CHUNK - 1.0 - idx)[None, :] * log_g[:, None])
    kv = jnp.einsum('bnjhd,hj,bnjhe->bnhde', kc, zeta, vc)
    chunk_decay = jnp.exp(CHUNK * log_g)[None, :, None, None]

    def step(state, kv_i):
        return state * chunk_decay + kv_i, state

    init = jnp.zeros((bsz, RET_HEADS, RET_QK_HEAD, RET_V_HEAD), f32)
    _, prev = lax.scan(step, init, jnp.moveaxis(kv, 1, 0))
    prev = jnp.moveaxis(prev, 0, 1)
    xi = jnp.exp((idx + 1.0)[None, :] * log_g[:, None])
    cross = jnp.einsum('bnihd,bnhde,hi->bnihe', qc, prev, xi)
    o = (inner + cross).reshape(bsz, seq, RET_HEADS, RET_V_HEAD)
    o = o * lax.rsqrt(jnp.mean(o * o, axis=-1, keepdims=True) + EPS)
    return o.reshape(bsz, seq, RET_WIDTH).astype(q.dtype)


def setup_inputs(seed: int = 0) -> dict:
    key = jax.random.key(seed)
    ks = jax.random.split(key, 20)
    f32 = jnp.float32
    nrm = lambda k, s: jax.random.normal(k, s, f32)
    x = nrm(ks[0], (BATCH, SEQ, D_MODEL))
    norm_pre = 1.0 + 0.01 * nrm(ks[1], (DEPTH, D_MODEL))
    w_in = nrm(ks[2], (DEPTH, D_MODEL, IN_COLS)) * D_MODEL ** -0.5
    lam_re = -0.5 + 0.01 * nrm(ks[3], (DEPTH, S5_GROUPS, S5_STATE))
    lam_im = (math.pi * jnp.arange(S5_STATE, dtype=f32))[None, None, :] + 0.01 * nrm(ks[4], (DEPTH, S5_GROUPS, S5_STATE))
    log_step = jax.random.uniform(ks[5], (DEPTH, S5_GROUPS), f32, math.log(S5_DT_MIN), math.log(S5_DT_MAX))
    s5_b_re = nrm(ks[6], (DEPTH, S5_GROUPS, S5_STATE, S5_GROUP)) * (2 * S5_GROUP) ** -0.5
    s5_b_im = nrm(ks[7], (DEPTH, S5_GROUPS, S5_STATE, S5_GROUP)) * (2 * S5_GROUP) ** -0.5
    s5_c_re = nrm(ks[8], (DEPTH, S5_GROUPS, S5_GROUP, S5_STATE)) * (2 * S5_STATE) ** -0.5
    s5_c_im = nrm(ks[9], (DEPTH, S5_GROUPS, S5_GROUP, S5_STATE)) * (2 * S5_STATE) ** -0.5
    s5_d = nrm(ks[10], (DEPTH, S5_WIDTH))
    w_glu = nrm(ks[11], (DEPTH, S5_WIDTH, S5_WIDTH)) * S5_WIDTH ** -0.5
    b_glu = 0.01 * nrm(ks[12], (DEPTH, S5_WIDTH))
    w_proj_s5 = nrm(ks[13], (DEPTH, S5_WIDTH, D_MODEL)) * S5_WIDTH ** -0.5
    w_proj_ret = nrm(ks[14], (DEPTH, RET_WIDTH, D_MODEL)) * RET_WIDTH ** -0.5
    w_out = nrm(ks[15], (DEPTH, D_MODEL, D_MODEL)) * D_MODEL ** -0.5
    norm_post = 1.0 + 0.01 * nrm(ks[16], (DEPTH, D_MODEL))
    return {"x": x, "norm_pre": norm_pre, "w_in": w_in, "lam_re": lam_re, "lam_im": lam_im,
            "log_step": log_step, "s5_b_re": s5_b_re, "s5_b_im": s5_b_im, "s5_c_re": s5_c_re,
            "s5_c_im": s5_c_im, "s5_d": s5_d, "w_glu": w_glu, "b_glu": b_glu,
            "w_proj_s5": w_proj_s5, "w_proj_ret": w_proj_ret, "w_out": w_out, "norm_post": norm_post}


def reference(x, norm_pre, w_in, lam_re, lam_im, log_step, s5_b_re, s5_b_im, s5_c_re, s5_c_im,
              s5_d, w_glu, b_glu, w_proj_s5, w_proj_ret, w_out, norm_post):
    for l in range(DEPTH):
        h = rms_norm(x, norm_pre[l])
        proj = h @ w_in[l]
        u_s, z_s, q, k, v, z_r, gl_s, gl_r = jnp.split(proj, SPLIT_IDX, axis=-1)
        y_s = s5_mixer(u_s, lam_re[l], lam_im[l], log_step[l], s5_b_re[l], s5_b_im[l],
                       s5_c_re[l], s5_c_im[l], s5_d[l], w_glu[l], b_glu[l]) * jax.nn.silu(z_s)
        y_r = retention(q, k, v) * jax.nn.silu(z_r)
        m = (jax.nn.sigmoid(gl_s) * (y_s @ w_proj_s5[l])
             + jax.nn.sigmoid(gl_r) * (y_r @ w_proj_ret[l]))
        x = x + rms_norm(m @ w_out[l], norm_post[l])
    return x
```

```python
import functools
import math

import numpy as np
import jax
import jax.numpy as jnp
from jax import lax
from jax.experimental import pallas as pl
from jax.experimental.pallas import tpu as pltpu

F32 = jnp.float32
BF16 = jnp.bfloat16

D_MODEL = 2048
EPS = 1e-6
CHUNK = 64

S5_WIDTH = 1024
S5_GROUP = 16
S5_GROUPS = 64
S5_STATE = 64

RET_HEADS = 8
RET_V_HEAD = 128
RET_QK_HEAD = 64
RET_QK = 512
RET_WIDTH = 1024
ROPE_BASE = 10000.0

IN_COLS = 2 * S5_WIDTH + 2 * RET_QK + 2 * RET_WIDTH + 2 * D_MODEL
_Q_BLK512 = (2 * S5_WIDTH) // 512
_K_BLK512 = (2 * S5_WIDTH + RET_QK) // 512
_US_BLK = 0
_ZS_BLK = 1
_V_BLK = (2 * S5_WIDTH + 2 * RET_QK) // 1024
_ZR_BLK = _V_BLK + 1
_GLS_BLK = _ZR_BLK + 1
_GLR_BLK = _GLS_BLK + 2

VMEM_LIMIT = 56 * 1024 * 1024

S5_GT = 8
S5_NT = S5_GROUPS // S5_GT
S5_TS = S5_GT * S5_STATE
S5_TC = S5_GT * S5_GROUP
S5_T = 256

RET_T = 256


def _dot(a, b):
    return jnp.dot(a, b, preferred_element_type=F32)


def _inproj_kernel(x_ref, g_ref, w_ref, o_ref, h_ref):
    @pl.when(pl.program_id(1) == 0)
    def _():
        x = x_ref[...]
        ms = jnp.mean(x * x, axis=-1, keepdims=True)
        h_ref[...] = (x * lax.rsqrt(ms + EPS) * g_ref[...]).astype(BF16)

    o_ref[...] = _dot(h_ref[...], w_ref[...]).astype(o_ref.dtype)


def _inproj(x2, gain, w_bf16, tm=1024, tn=1024):
    seq, d = x2.shape
    n = w_bf16.shape[1]
    return pl.pallas_call(
        _inproj_kernel,
        grid=(seq // tm, n // tn),
        in_specs=[
            pl.BlockSpec((tm, d), lambda i, j: (i, 0)),
            pl.BlockSpec((1, d), lambda i, j: (0, 0)),
            pl.BlockSpec((d, tn), lambda i, j: (0, j)),
        ],
        out_specs=pl.BlockSpec((tm, tn), lambda i, j: (i, j)),
        out_shape=jax.ShapeDtypeStruct((seq, n), BF16),
        scratch_shapes=[pltpu.VMEM((tm, d), BF16)],
        compiler_params=pltpu.CompilerParams(
            dimension_semantics=("parallel", "arbitrary"), vmem_limit_bytes=VMEM_LIMIT),
        name="inproj",
    )(x2, gain.reshape(1, d).astype(F32), w_bf16)


def _s5_tables(lam_re, lam_im, log_step, b_re, b_im, c_re, c_im):
    t = S5_T
    mid = t // 2
    lr = lam_re.astype(F32)
    li = lam_im.astype(F32)
    step = jnp.exp(log_step.astype(F32))[:, None]
    mag = jnp.exp(lr * step)
    ang = li * step
    ab_re = mag * jnp.cos(ang)
    ab_im = mag * jnp.sin(ang)
    den = lr * lr + li * li
    nr = ab_re - 1.0
    f_re = (nr * lr + ab_im * li) / den
    f_im = (ab_im * lr - nr * li) / den
    br = b_re.astype(F32)
    bi = b_im.astype(F32)
    bb_re = f_re[..., None] * br - f_im[..., None] * bi
    bb_im = f_re[..., None] * bi + f_im[..., None] * br

    def powers(expo):
        e = expo.astype(F32)[:, None, None]
        m = jnp.exp(e * (lr * step)[None])
        th = e * ang[None]
        n = expo.shape[0]
        return ((m * jnp.cos(th)).reshape(n, S5_GROUPS * S5_STATE),
                (m * jnp.sin(th)).reshape(n, S5_GROUPS * S5_STATE))

    idx = jnp.arange(t, dtype=jnp.int32)
    p_re, p_im = powers(mid - idx)
    q_re, q_im = powers(idx - mid)
    a1_re, a1_im = powers(jnp.array([mid + 1], dtype=jnp.int32))

    eye = jnp.eye(S5_GT, dtype=F32)

    def in_blockdiag(bb):
        w = jnp.transpose(bb, (0, 2, 1)).reshape(S5_NT, S5_GT, S5_GROUP, S5_STATE)
        w = jnp.einsum('tghp,gk->tghkp', w, eye)
        return w.reshape(S5_NT, S5_TC, S5_TS)

    def out_blockdiag(c):
        w = jnp.transpose(c, (0, 2, 1)).reshape(S5_NT, S5_GT, S5_STATE, S5_GROUP)
        w = jnp.einsum('tgph,gk->tgpkh', w, eye)
        return w.reshape(S5_NT, S5_TS, S5_TC)

    w_b = jnp.concatenate([in_blockdiag(bb_re), in_blockdiag(bb_im)], axis=2).astype(BF16)
    w_c = jnp.concatenate([out_blockdiag(c_re.astype(F32)), -out_blockdiag(c_im.astype(F32))],
                          axis=1).astype(BF16)
    tri = (idx[:, None] >= idx[None, :]).astype(BF16)
    return w_b, w_c, p_re, p_im, q_re, q_im, a1_re, a1_im, tri


def _cmul(ar, ai, br, bi):
    return ar * br - ai * bi, ar * bi + ai * br


def _gelu_tanh(y):
    return 0.5 * y * (1.0 + jnp.tanh(math.sqrt(2.0 / math.pi) * (y + 0.044715 * (y * y * y))))


def _s5_kernel(u_ref, z_ref, wb_ref, wc_ref, pre_ref, pim_ref, qre_ref, qim_ref, a1re_ref, a1im_ref,
               tri_ref, d_ref, wglu_ref, bglu_ref, o_ref, carry_ref):
    @pl.when(pl.program_id(0) == 0)
    def _():
        carry_ref[...] = jnp.zeros_like(carry_ref)

    t = S5_T
    u = u_ref[...]
    tri = tri_ref[...]
    first_row = lax.broadcasted_iota(jnp.int32, (16, S5_TS), 0) == 0
    ys = []
    for gt in range(S5_NT):
        cs = slice(gt * S5_TS, (gt + 1) * S5_TS)
        bu = _dot(u[:, gt * S5_TC:(gt + 1) * S5_TC], wb_ref[gt])
        vr, vi = _cmul(pre_ref[:, cs], pim_ref[:, cs], bu[:, :S5_TS], bu[:, S5_TS:])
        cr, ci = _cmul(a1re_ref[:, cs], a1im_ref[:, cs], carry_ref[0:1, cs], carry_ref[1:2, cs])
        vr = jnp.concatenate([vr[:16] + jnp.where(first_row, cr, 0.0), vr[16:]], axis=0)
        vi = jnp.concatenate([vi[:16] + jnp.where(first_row, ci, 0.0), vi[16:]], axis=0)
        v = jnp.concatenate([vr, vi], axis=1).astype(BF16)
        z = _dot(tri, v)
        xr, xi = _cmul(qre_ref[:, cs], qim_ref[:, cs], z[:, :S5_TS], z[:, S5_TS:])
        carry_ref[0:1, cs] = xr[t - 1:t]
        carry_ref[1:2, cs] = xi[t - 1:t]
        xs = jnp.concatenate([xr, xi], axis=1).astype(BF16)
        ys.append(_dot(xs, wc_ref[gt]))
    y = jnp.concatenate(ys, axis=1) + d_ref[...] * u.astype(F32)
    g = _gelu_tanh(y)
    gate = jax.nn.sigmoid(_dot(g.astype(BF16), wglu_ref[...]) + bglu_ref[...])
    o_ref[...] = (g * gate * jax.nn.silu(z_ref[...].astype(F32))).astype(o_ref.dtype)


def _s5(proj, tables, d_skip, w_glu_bf16, b_glu):
    seq = proj.shape[0]
    w_b, w_c, p_re, p_im, q_re, q_im, a1_re, a1_im, tri = tables
    t = S5_T
    ns = S5_GROUPS * S5_STATE
    const2 = lambda i: (0, 0)
    const3 = lambda i: (0, 0, 0)
    return pl.pallas_call(
        _s5_kernel,
        grid=(seq // t,),
        in_specs=[
            pl.BlockSpec((t, S5_WIDTH), lambda i: (i, _US_BLK)),
            pl.BlockSpec((t, S5_WIDTH), lambda i: (i, _ZS_BLK)),
            pl.BlockSpec(w_b.shape, const3),
            pl.BlockSpec(w_c.shape, const3),
            pl.BlockSpec((t, ns), const2),
            pl.BlockSpec((t, ns), const2),
            pl.BlockSpec((t, ns), const2),
            pl.BlockSpec((t, ns), const2),
            pl.BlockSpec((1, ns), const2),
            pl.BlockSpec((1, ns), const2),
            pl.BlockSpec((t, t), const2),
            pl.BlockSpec((1, S5_WIDTH), const2),
            pl.BlockSpec((S5_WIDTH, S5_WIDTH), const2),
            pl.BlockSpec((1, S5_WIDTH), const2),
        ],
        out_specs=pl.BlockSpec((t, S5_WIDTH), lambda i: (i, 0)),
        out_shape=jax.ShapeDtypeStruct((seq, S5_WIDTH), BF16),
        scratch_shapes=[pltpu.VMEM((2, ns), F32)],
        compiler_params=pltpu.CompilerParams(
            dimension_semantics=("arbitrary",), vmem_limit_bytes=VMEM_LIMIT),
        name="s5_mixer",
    )(proj, proj, w_b, w_c, p_re, p_im, q_re, q_im, a1_re, a1_im, tri,
      d_skip.reshape(1, S5_WIDTH).astype(F32), w_glu_bf16, b_glu.reshape(1, S5_WIDTH).astype(F32))


def _ret_tables(seq):
    half = RET_QK_HEAD // 2
    inv = ROPE_BASE ** (-jnp.arange(0, RET_QK_HEAD, 2, dtype=F32) / RET_QK_HEAD)
    pos = jnp.arange(seq, dtype=F32)
    angle = pos[:, None] * inv[None, :]
    cos = jnp.cos(angle)
    sin = jnp.sin(angle)
    cos2 = jnp.concatenate([cos, cos, cos, cos], axis=1)
    sin2 = jnp.concatenate([-sin, sin, -sin, sin], axis=1)

    t = RET_T
    log_g = np.log1p(-np.exp2(-5.0 - np.arange(RET_HEADS, dtype=np.float64)))
    i = np.arange(t)
    ci = i // CHUNK
    diff = (i[:, None] - i[None, :]).astype(np.float64)
    same = ci[:, None] == ci[None, :]
    earlier = ci[None, :] < ci[:, None]
    expo = np.where(same, np.abs(diff), diff)
    dmask = np.where((same | earlier)[None], np.exp(expo[None] * log_g[:, None, None]), 0.0)
    xi = np.exp((i + 1.0)[None, :] * log_g[:, None])
    zeta = np.exp((t - 1.0 - i)[None, :] * log_g[:, None])
    xi = np.broadcast_to(xi[:, :, None], (RET_HEADS, t, RET_V_HEAD))
    zeta = np.broadcast_to(zeta[:, :, None], (RET_HEADS, t, RET_V_HEAD))
    block_decay = [float(v) for v in np.exp(t * log_g)]
    return (cos2, sin2, jnp.asarray(dmask, F32), jnp.asarray(xi, F32), jnp.asarray(zeta, F32),
            block_decay)


def _swap_halves(x):
    n = x.shape[1]
    half = RET_QK_HEAD // 2
    fwd = pltpu.roll(x, half, 1)
    bwd = pltpu.roll(x, n - half, 1)
    lane = lax.broadcasted_iota(jnp.int32, x.shape, 1)
    return jnp.where((lane & (RET_QK_HEAD - 1)) < half, bwd, fwd)


def _ret_kernel(block_decay, q_ref, k_ref, v_ref, z_ref, cos_ref, sin_ref, dm_ref, xi_ref, zeta_ref,
                o_ref, state_ref):
    @pl.when(pl.program_id(0) == 0)
    def _():
        state_ref[...] = jnp.zeros_like(state_ref)

    reps = RET_QK // cos_ref.shape[1]
    cos = jnp.concatenate([cos_ref[...]] * reps, axis=1)
    sin = jnp.concatenate([sin_ref[...]] * reps, axis=1)
    q = q_ref[...].astype(F32)
    k = k_ref[...].astype(F32)
    qr = (q * cos + _swap_halves(q) * sin).astype(BF16)
    kr = ((k * cos + _swap_halves(k) * sin) * (RET_QK_HEAD ** -0.5)).astype(BF16)
    for h in range(RET_HEADS):
        qs = slice(h * RET_QK_HEAD, (h + 1) * RET_QK_HEAD)
        vs = slice(h * RET_V_HEAD, (h + 1) * RET_V_HEAD)
        qh = qr[:, qs]
        kh = kr[:, qs]
        vh = v_ref[:, vs]
        s = lax.dot_general(qh, kh, (((1,), (1,)), ((), ())), preferred_element_type=F32)
        s = s * dm_ref[h]
        inner = _dot(s.astype(BF16), vh)
        st = state_ref[h]
        cross = _dot(qh, st.astype(BF16)) * xi_ref[h]
        vz = (vh.astype(F32) * zeta_ref[h]).astype(BF16)
        kv = lax.dot_general(kh, vz, (((0,), (0,)), ((), ())), preferred_element_type=F32)
        state_ref[h] = st * block_decay[h] + kv
        o = inner + cross
        o = o * lax.rsqrt(jnp.mean(o * o, axis=-1, keepdims=True) + EPS)
        o_ref[:, vs] = (o * jax.nn.silu(z_ref[:, vs].astype(F32))).astype(o_ref.dtype)


def _retention(proj):
    seq = proj.shape[0]
    t = RET_T
    cos2, sin2, dmask, xi, zeta, block_decay = _ret_tables(seq)
    const3 = lambda i: (0, 0, 0)
    return pl.pallas_call(
        functools.partial(_ret_kernel, block_decay),
        grid=(seq // t,),
        in_specs=[
            pl.BlockSpec((t, RET_QK), lambda i: (i, _Q_BLK512)),
            pl.BlockSpec((t, RET_QK), lambda i: (i, _K_BLK512)),
            pl.BlockSpec((t, RET_WIDTH), lambda i: (i, _V_BLK)),
            pl.BlockSpec((t, RET_WIDTH), lambda i: (i, _ZR_BLK)),
            pl.BlockSpec((t, cos2.shape[1]), lambda i: (i, 0)),
            pl.BlockSpec((t, sin2.shape[1]), lambda i: (i, 0)),
            pl.BlockSpec(dmask.shape, const3),
            pl.BlockSpec(xi.shape, const3),
            pl.BlockSpec(zeta.shape, const3),
        ],
        out_specs=pl.BlockSpec((t, RET_WIDTH), lambda i: (i, 0)),
        out_shape=jax.ShapeDtypeStruct((seq, RET_WIDTH), BF16),
        scratch_shapes=[pltpu.VMEM((RET_HEADS, RET_QK_HEAD, RET_V_HEAD), F32)],
        compiler_params=pltpu.CompilerParams(
            dimension_semantics=("arbitrary",), vmem_limit_bytes=VMEM_LIMIT),
        name="retention",
    )(proj, proj, proj, proj, cos2, sin2, dmask, xi, zeta)


def _out_kernel(ys_ref, yr_ref, gs0_ref, gs1_ref, gr0_ref, gr1_ref, x_ref, ws_ref, wr_ref, wo_ref,
                g_ref, o_ref):
    a = _dot(ys_ref[...], ws_ref[...])
    b = _dot(yr_ref[...], wr_ref[...])
    gs = jnp.concatenate([gs0_ref[...], gs1_ref[...]], axis=1).astype(F32)
    gr = jnp.concatenate([gr0_ref[...], gr1_ref[...]], axis=1).astype(F32)
    m = jax.nn.sigmoid(gs) * a + jax.nn.sigmoid(gr) * b
    o = _dot(m.astype(BF16), wo_ref[...])
    o = o * lax.rsqrt(jnp.mean(o * o, axis=-1, keepdims=True) + EPS) * g_ref[...]
    o_ref[...] = x_ref[...] + o


def _out_stage(y_s, y_r, proj, x2, w_s, w_r, w_o, gain, tm=256):
    seq, d = x2.shape
    const2 = lambda i: (0, 0)
    single = pl.Buffered(1)
    blk = 1024
    return pl.pallas_call(
        _out_kernel,
        grid=(seq // tm,),
        in_specs=[
            pl.BlockSpec((tm, S5_WIDTH), lambda i: (i, 0)),
            pl.BlockSpec((tm, RET_WIDTH), lambda i: (i, 0)),
            pl.BlockSpec((tm, blk), lambda i: (i, _GLS_BLK)),
            pl.BlockSpec((tm, blk), lambda i: (i, _GLS_BLK + 1)),
            pl.BlockSpec((tm, blk), lambda i: (i, _GLR_BLK)),
            pl.BlockSpec((tm, blk), lambda i: (i, _GLR_BLK + 1)),
            pl.BlockSpec((tm, d), lambda i: (i, 0)),
            pl.BlockSpec(w_s.shape, const2, pipeline_mode=single),
            pl.BlockSpec(w_r.shape, const2, pipeline_mode=single),
            pl.BlockSpec(w_o.shape, const2, pipeline_mode=single),
            pl.BlockSpec((1, d), const2),
        ],
        out_specs=pl.BlockSpec((tm, d), lambda i: (i, 0)),
        out_shape=jax.ShapeDtypeStruct((seq, d), F32),
        compiler_params=pltpu.CompilerParams(
            dimension_semantics=("parallel",), vmem_limit_bytes=VMEM_LIMIT),
        name="out_stage",
    )(y_s, y_r, proj, proj, proj, proj, x2, w_s, w_r, w_o, gain.reshape(1, d).astype(F32))


def kernel(x, norm_pre, w_in, lam_re, lam_im, log_step, s5_b_re, s5_b_im, s5_c_re, s5_c_im, s5_d,
           w_glu, b_glu, w_proj_s5, w_proj_ret, w_out, norm_post):
    bsz, seq, d = x.shape
    assert bsz == 1 and d == D_MODEL and seq % 1024 == 0
    depth = w_in.shape[0]
    x2 = x.reshape(seq, d)
    for l in range(depth):
        proj = _inproj(x2, norm_pre[l], w_in[l].astype(BF16))
        tables = _s5_tables(lam_re[l], lam_im[l], log_step[l], s5_b_re[l], s5_b_im[l],
                            s5_c_re[l], s5_c_im[l])
        y_s = _s5(proj, tables, s5_d[l], w_glu[l].astype(BF16), b_glu[l])
        y_r = _retention(proj)
        x2 = _out_stage(y_s, y_r, proj, x2, w_proj_s5[l].astype(BF16), w_proj_ret[l].astype(BF16),
                        w_out[l].astype(BF16), norm_post[l])
    return x2.reshape(bsz, seq, d)
```

```python
import functools
import math

import numpy as np
import jax
import jax.numpy as jnp
from jax import lax
from jax.experimental import pallas as pl
from jax.experimental.pallas import tpu as pltpu

F32 = jnp.float32
BF16 = jnp.bfloat16
HIGHEST = lax.Precision.HIGHEST

D_MODEL = 2048
EPS = 1e-6
CHUNK = 64

S5_WIDTH = 1024
S5_GROUP = 16
S5_GROUPS = 64
S5_STATE = 64

RET_HEADS = 8
RET_V_HEAD = 128
RET_QK_HEAD = 64
RET_QK = 512
RET_WIDTH = 1024
ROPE_BASE = 10000.0

IN_COLS = 2 * S5_WIDTH + 2 * RET_QK + 2 * RET_WIDTH + 2 * D_MODEL
_Q_BLK512 = (2 * S5_WIDTH) // 512
_K_BLK512 = (2 * S5_WIDTH + RET_QK) // 512
_ZS_BLK = 1
_V_BLK = (2 * S5_WIDTH + 2 * RET_QK) // 1024
_ZR_BLK = _V_BLK + 1
_GLS_BLK = _ZR_BLK + 1
_GLR_BLK = _GLS_BLK + 2

VMEM_LIMIT = 56 * 1024 * 1024

S5_T = 32
S5_GB = 8
S5_TW = S5_T * S5_GROUP

RET_T = 256


def _dot(a, b):
    return jnp.dot(a, b, preferred_element_type=F32)


def _inproj_kernel(x_ref, g_ref, w_ref, o_ref, h_ref):
    @pl.when(pl.program_id(1) == 0)
    def _():
        x = x_ref[...]
        ms = jnp.mean(x * x, axis=-1, keepdims=True)
        h_ref[...] = (x * lax.rsqrt(ms + EPS) * g_ref[...]).astype(BF16)

    o_ref[...] = _dot(h_ref[...], w_ref[...]).astype(o_ref.dtype)


def _inproj(x2, gain, w_bf16, tm=1024, tn=1024):
    seq, d = x2.shape
    n = w_bf16.shape[1]
    return pl.pallas_call(
        _inproj_kernel,
        grid=(seq // tm, n // tn),
        in_specs=[
            pl.BlockSpec((tm, d), lambda i, j: (i, 0)),
            pl.BlockSpec((1, d), lambda i, j: (0, 0)),
            pl.BlockSpec((d, tn), lambda i, j: (0, j)),
        ],
        out_specs=pl.BlockSpec((tm, tn), lambda i, j: (i, j)),
        out_shape=jax.ShapeDtypeStruct((seq, n), BF16),
        scratch_shapes=[pltpu.VMEM((tm, d), BF16)],
        compiler_params=pltpu.CompilerParams(
            dimension_semantics=("parallel", "arbitrary"), vmem_limit_bytes=VMEM_LIMIT),
        name="inproj",
    )(x2, gain.reshape(1, d).astype(F32), w_bf16)


def _s5_tables(lam_re, lam_im, log_step, b_re, b_im, c_re, c_im, d_skip, n_chunks):
    t = S5_T
    lr = lam_re.astype(F32)
    li = lam_im.astype(F32)
    step = jnp.exp(log_step.astype(F32))[:, None]
    mag = jnp.exp(lr * step)
    ang = li * step
    ab_re = mag * jnp.cos(ang)
    ab_im = mag * jnp.sin(ang)
    den = lr * lr + li * li
    nr = ab_re - 1.0
    f_re = (nr * lr + ab_im * li) / den
    f_im = (ab_im * lr - nr * li) / den
    br = b_re.astype(F32)
    bi = b_im.astype(F32)
    bb_re = f_re[..., None] * br - f_im[..., None] * bi
    bb_im = f_re[..., None] * bi + f_im[..., None] * br
    cr = c_re.astype(F32)
    ci = c_im.astype(F32)

    expo = jnp.arange(t + 1, dtype=F32)[:, None, None]
    pm = jnp.exp(expo * (lr * step)[None])
    pw_re = pm * jnp.cos(expo * ang[None])
    pw_im = pm * jnp.sin(expo * ang[None])

    def c_times_pow(lo):
        pr = jnp.transpose(pw_re[lo:lo + t], (1, 0, 2))[:, :, None, :]
        pi = jnp.transpose(pw_im[lo:lo + t], (1, 0, 2))[:, :, None, :]
        return cr[:, None] * pr - ci[:, None] * pi, cr[:, None] * pi + ci[:, None] * pr

    ca_re, ca_im = c_times_pow(0)
    k_lag = (jnp.einsum('gdhp,gpk->gdhk', ca_re, bb_re, precision=HIGHEST)
             - jnp.einsum('gdhp,gpk->gdhk', ca_im, bb_im, precision=HIGHEST))
    skip = d_skip.astype(F32).reshape(S5_GROUPS, S5_GROUP)
    k_lag = k_lag.at[:, 0].add(skip[:, :, None] * jnp.eye(S5_GROUP, dtype=F32))
    strip = jnp.transpose(k_lag[:, ::-1], (0, 2, 1, 3)).reshape(S5_GROUPS, S5_GROUP, S5_TW)
    strip = jnp.concatenate([strip, jnp.zeros_like(strip)], axis=2)

    qr = jnp.transpose(pw_re[:t][::-1], (1, 2, 0))[..., None]
    qi = jnp.transpose(pw_im[:t][::-1], (1, 2, 0))[..., None]
    ws_re = (qr * bb_re[:, :, None, :] - qi * bb_im[:, :, None, :]).reshape(S5_GROUPS, S5_STATE, S5_TW)
    ws_im = (qr * bb_im[:, :, None, :] + qi * bb_re[:, :, None, :]).reshape(S5_GROUPS, S5_STATE, S5_TW)
    w_s = jnp.concatenate([ws_re, ws_im], axis=1).astype(BF16)

    co_re, co_im = c_times_pow(1)
    w_o = jnp.concatenate([co_re.reshape(S5_GROUPS, S5_TW, S5_STATE),
                           -co_im.reshape(S5_GROUPS, S5_TW, S5_STATE)], axis=2).astype(BF16)

    n_steps = max(1, int(math.ceil(math.log2(n_chunks))))
    sq_re, sq_im = [pw_re[t]], [pw_im[t]]
    for _ in range(n_steps - 1):
        r, i = sq_re[-1], sq_im[-1]
        sq_re.append(r * r - i * i)
        sq_im.append(2.0 * r * i)
    ap_re = jnp.stack(sq_re, axis=-1)
    ap_im = jnp.stack(sq_im, axis=-1)
    return strip, w_s, w_o, ap_re, ap_im


def _s5core_kernel(n_steps, u_ref, strip_ref, ws_ref, wo_ref, apr_ref, api_ref, o_ref,
                   time_scr, rhs_scr, mt_scr, yt_scr):
    t = S5_T
    nch = time_scr.shape[0] // t
    time_scr[...] = u_ref[...].astype(F32)
    for i in range(t):
        ui = time_scr[pl.ds(i, nch, stride=t), :]
        rhs_scr[:, i * S5_GROUP:(i + 1) * S5_GROUP, :] = (
            ui.T.astype(BF16).reshape(S5_GB, S5_GROUP, nch))

    lane = lax.broadcasted_iota(jnp.int32, (S5_STATE, nch), 1)

    def shift_right(v, k):
        return jnp.where(lane >= k, pltpu.roll(v, k, 1), 0.0)

    def per_group(g, carry):
        for i in range(t):
            lo = (t - 1 - i) * S5_GROUP
            mt_scr[i * S5_GROUP:(i + 1) * S5_GROUP, :] = strip_ref[g, :, lo:lo + S5_TW].astype(BF16)
        rhs = rhs_scr[g]
        yt = _dot(mt_scr[...], rhs)
        s_loc = _dot(ws_ref[g], rhs)
        xr = s_loc[:S5_STATE]
        xi = s_loc[S5_STATE:]
        apr = apr_ref[g]
        api = api_ref[g]
        for k in range(n_steps):
            ar = apr[:, k:k + 1]
            ai = api[:, k:k + 1]
            sr = shift_right(xr, 1 << k)
            si = shift_right(xi, 1 << k)
            xr, xi = xr + (ar * sr - ai * si), xi + (ar * si + ai * sr)
        xprev = jnp.concatenate([shift_right(xr, 1), shift_right(xi, 1)], axis=0).astype(BF16)
        yt_scr[g] = yt + _dot(wo_ref[g], xprev)
        return carry

    lax.fori_loop(0, S5_GB, per_group, 0)

    for i in range(t):
        blk = yt_scr[:, i * S5_GROUP:(i + 1) * S5_GROUP, :].reshape(S5_GB * S5_GROUP, nch)
        time_scr[pl.ds(i, nch, stride=t), :] = blk.T
    o_ref[...] = time_scr[...].astype(o_ref.dtype)


def _s5core(proj, tables):
    seq = proj.shape[0]
    strip, w_s, w_o, ap_re, ap_im = tables
    nch = seq // S5_T
    n_steps = ap_re.shape[-1]
    lanes = S5_GB * S5_GROUP
    grp = lambda i: (i, 0, 0)
    return pl.pallas_call(
        functools.partial(_s5core_kernel, n_steps),
        grid=(S5_GROUPS // S5_GB,),
        in_specs=[
            pl.BlockSpec((seq, lanes), lambda i: (0, i)),
            pl.BlockSpec((S5_GB,) + strip.shape[1:], grp),
            pl.BlockSpec((S5_GB,) + w_s.shape[1:], grp),
            pl.BlockSpec((S5_GB,) + w_o.shape[1:], grp),
            pl.BlockSpec((S5_GB,) + ap_re.shape[1:], grp),
            pl.BlockSpec((S5_GB,) + ap_im.shape[1:], grp),
        ],
        out_specs=pl.BlockSpec((seq, lanes), lambda i: (0, i)),
        out_shape=jax.ShapeDtypeStruct((seq, S5_WIDTH), BF16),
        scratch_shapes=[
            pltpu.VMEM((seq, lanes), F32),
            pltpu.VMEM((S5_GB, S5_TW, nch), BF16),
            pltpu.VMEM((S5_TW, S5_TW), BF16),
            pltpu.VMEM((S5_GB, S5_TW, nch), F32),
        ],
        compiler_params=pltpu.CompilerParams(
            dimension_semantics=("parallel",), vmem_limit_bytes=VMEM_LIMIT),
        name="s5_core",
    )(proj, strip, w_s, w_o, ap_re, ap_im)


def _gelu_tanh(y):
    return 0.5 * y * (1.0 + jnp.tanh(math.sqrt(2.0 / math.pi) * (y + 0.044715 * (y * y * y))))


def _s5post_kernel(y_ref, z_ref, wglu_ref, bglu_ref, o_ref):
    g = _gelu_tanh(y_ref[...].astype(F32))
    gate = jax.nn.sigmoid(_dot(g.astype(BF16), wglu_ref[...]) + bglu_ref[...])
    o_ref[...] = (g * gate * jax.nn.silu(z_ref[...].astype(F32))).astype(o_ref.dtype)


def _s5post(y, proj, w_glu_bf16, b_glu, tm=512):
    seq = y.shape[0]
    const2 = lambda i: (0, 0)
    return pl.pallas_call(
        _s5post_kernel,
        grid=(seq // tm,),
        in_specs=[
            pl.BlockSpec((tm, S5_WIDTH), lambda i: (i, 0)),
            pl.BlockSpec((tm, S5_WIDTH), lambda i: (i, _ZS_BLK)),
            pl.BlockSpec((S5_WIDTH, S5_WIDTH), const2),
            pl.BlockSpec((1, S5_WIDTH), const2),
        ],
        out_specs=pl.BlockSpec((tm, S5_WIDTH), lambda i: (i, 0)),
        out_shape=jax.ShapeDtypeStruct((seq, S5_WIDTH), BF16),
        compiler_params=pltpu.CompilerParams(
            dimension_semantics=("parallel",), vmem_limit_bytes=VMEM_LIMIT),
        name="s5_post",
    )(y, proj, w_glu_bf16, b_glu.reshape(1, S5_WIDTH).astype(F32))


def _ret_tables(seq):
    t = RET_T
    inv = ROPE_BASE ** (-jnp.arange(0, RET_QK_HEAD, 2, dtype=F32) / RET_QK_HEAD)
    tile4 = lambda a: jnp.concatenate([a, a, a, a], axis=-1)
    ang_r = jnp.arange(t, dtype=F32)[:, None] * inv[None, :]
    ang_b = (jnp.arange(seq // t, dtype=F32) * float(t))[:, None] * inv[None, :]
    cos_r, sin_r = tile4(jnp.cos(ang_r)), tile4(jnp.sin(ang_r))
    cos_b, sin_b = tile4(jnp.cos(ang_b))[:, None, :], tile4(jnp.sin(ang_b))[:, None, :]
    half = RET_QK_HEAD // 2
    sign = jnp.asarray(np.tile(np.repeat([-1.0, 1.0], half), 2)[None, :], F32)

    log_g = np.log1p(-np.exp2(-5.0 - np.arange(RET_HEADS, dtype=np.float64)))
    i = np.arange(t)
    ci = i // CHUNK
    diff = (i[:, None] - i[None, :]).astype(np.float64)
    same = ci[:, None] == ci[None, :]
    earlier = ci[None, :] < ci[:, None]
    expo = np.where(same, np.abs(diff), diff)
    dmask = np.where((same | earlier)[None], np.exp(expo[None] * log_g[:, None, None]), 0.0)
    xi = np.exp((i + 1.0)[None, :] * log_g[:, None])
    zeta = np.exp((t - 1.0 - i)[None, :] * log_g[:, None])
    xi = np.broadcast_to(xi[:, :, None], (RET_HEADS, t, RET_V_HEAD))
    zeta = np.broadcast_to(zeta[:, :, None], (RET_HEADS, t, RET_V_HEAD))
    block_decay = [float(v) for v in np.exp(t * log_g)]
    return (cos_r, sin_r, cos_b, sin_b, sign, jnp.asarray(dmask, F32), jnp.asarray(xi, F32),
            jnp.asarray(zeta, F32), block_decay)


def _swap_halves(x):
    n = x.shape[1]
    half = RET_QK_HEAD // 2
    fwd = pltpu.roll(x, half, 1)
    bwd = pltpu.roll(x, n - half, 1)
    lane = lax.broadcasted_iota(jnp.int32, x.shape, 1)
    return jnp.where((lane & (RET_QK_HEAD - 1)) < half, bwd, fwd)


def _ret_kernel(block_decay, q_ref, k_ref, v_ref, z_ref, cosr_ref, sinr_ref, cosb_ref, sinb_ref,
                sign_ref, dm_ref, xi_ref, zeta_ref, o_ref, state_ref):
    @pl.when(pl.program_id(0) == 0)
    def _():
        state_ref[...] = jnp.zeros_like(state_ref)

    cb = cosb_ref[...]
    sb = sinb_ref[...]
    cos2 = cb * cosr_ref[...] - sb * sinr_ref[...]
    sin2 = (sb * cosr_ref[...] + cb * sinr_ref[...]) * sign_ref[...]
    reps = RET_QK // cos2.shape[1]
    cos = jnp.concatenate([cos2] * reps, axis=1)
    sin = jnp.concatenate([sin2] * reps, axis=1)
    q = q_ref[...].astype(F32)
    k = k_ref[...].astype(F32)
    qr = (q * cos + _swap_halves(q) * sin).astype(BF16)
    kr = ((k * cos + _swap_halves(k) * sin) * (RET_QK_HEAD ** -0.5)).astype(BF16)
    for h in range(RET_HEADS):
        qs = slice(h * RET_QK_HEAD, (h + 1) * RET_QK_HEAD)
        vs = slice(h * RET_V_HEAD, (h + 1) * RET_V_HEAD)
        qh = qr[:, qs]
        kh = kr[:, qs]
        vh = v_ref[:, vs]
        s = lax.dot_general(qh, kh, (((1,), (1,)), ((), ())), preferred_element_type=F32)
        s = s * dm_ref[h]
        inner = _dot(s.astype(BF16), vh)
        st = state_ref[h]
        cross = _dot(qh, st.astype(BF16)) * xi_ref[h]
        vz = (vh.astype(F32) * zeta_ref[h]).astype(BF16)
        kv = lax.dot_general(kh, vz, (((0,), (0,)), ((), ())), preferred_element_type=F32)
        state_ref[h] = st * block_decay[h] + kv
        o = inner + cross
        o = o * lax.rsqrt(jnp.mean(o * o, axis=-1, keepdims=True) + EPS)
        o_ref[:, vs] = (o * jax.nn.silu(z_ref[:, vs].astype(F32))).astype(o_ref.dtype)


def _retention(proj):
    seq = proj.shape[0]
    t = RET_T
    cos_r, sin_r, cos_b, sin_b, sign, dmask, xi, zeta, block_decay = _ret_tables(seq)
    const2 = lambda i: (0, 0)
    const3 = lambda i: (0, 0, 0)
    return pl.pallas_call(
        functools.partial(_ret_kernel, block_decay),
        grid=(seq // t,),
        in_specs=[
            pl.BlockSpec((t, RET_QK), lambda i: (i, _Q_BLK512)),
            pl.BlockSpec((t, RET_QK), lambda i: (i, _K_BLK512)),
            pl.BlockSpec((t, RET_WIDTH), lambda i: (i, _V_BLK)),
            pl.BlockSpec((t, RET_WIDTH), lambda i: (i, _ZR_BLK)),
            pl.BlockSpec(cos_r.shape, const2),
            pl.BlockSpec(sin_r.shape, const2),
            pl.BlockSpec((None, 1, cos_b.shape[2]), lambda i: (i, 0, 0)),
            pl.BlockSpec((None, 1, sin_b.shape[2]), lambda i: (i, 0, 0)),
            pl.BlockSpec(sign.shape, const2),
            pl.BlockSpec(dmask.shape, const3),
            pl.BlockSpec(xi.shape, const3),
            pl.BlockSpec(zeta.shape, const3),
        ],
        out_specs=pl.BlockSpec((t, RET_WIDTH), lambda i: (i, 0)),
        out_shape=jax.ShapeDtypeStruct((seq, RET_WIDTH), BF16),
        scratch_shapes=[pltpu.VMEM((RET_HEADS, RET_QK_HEAD, RET_V_HEAD), F32)],
        compiler_params=pltpu.CompilerParams(
            dimension_semantics=("arbitrary",), vmem_limit_bytes=VMEM_LIMIT),
        name="retention",
    )(proj, proj, proj, proj, cos_r, sin_r, cos_b, sin_b, sign, dmask, xi, zeta)


def _out_kernel(ys_ref, yr_ref, gs0_ref, gs1_ref, gr0_ref, gr1_ref, x_ref, ws_ref, wr_ref, wo_ref,
                g_ref, o_ref):
    a = _dot(ys_ref[...], ws_ref[...])
    b = _dot(yr_ref[...], wr_ref[...])
    gs = jnp.concatenate([gs0_ref[...], gs1_ref[...]], axis=1).astype(F32)
    gr = jnp.concatenate([gr0_ref[...], gr1_ref[...]], axis=1).astype(F32)
    m = jax.nn.sigmoid(gs) * a + jax.nn.sigmoid(gr) * b
    o = _dot(m.astype(BF16), wo_ref[...])
    o = o * lax.rsqrt(jnp.mean(o * o, axis=-1, keepdims=True) + EPS) * g_ref[...]
    o_ref[...] = x_ref[...] + o


def _out_stage(y_s, y_r, proj, x2, w_s, w_r, w_o, gain, tm=256):
    seq, d = x2.shape
    const2 = lambda i: (0, 0)
    single = pl.Buffered(1)
    blk = 1024
    return pl.pallas_call(
        _out_kernel,
        grid=(seq // tm,),
        in_specs=[
            pl.BlockSpec((tm, S5_WIDTH), lambda i: (i, 0)),
            pl.BlockSpec((tm, RET_WIDTH), lambda i: (i, 0)),
            pl.BlockSpec((tm, blk), lambda i: (i, _GLS_BLK)),
            pl.BlockSpec((tm, blk), lambda i: (i, _GLS_BLK + 1)),
            pl.BlockSpec((tm, blk), lambda i: (i, _GLR_BLK)),
            pl.BlockSpec((tm, blk), lambda i: (i, _GLR_BLK + 1)),
            pl.BlockSpec((tm, d), lambda i: (i, 0)),
            pl.BlockSpec(w_s.shape, const2, pipeline_mode=single),
            pl.BlockSpec(w_r.shape, const2, pipeline_mode=single),
            pl.BlockSpec(w_o.shape, const2, pipeline_mode=single),
            pl.BlockSpec((1, d), const2),
        ],
        out_specs=pl.BlockSpec((tm, d), lambda i: (i, 0)),
        out_shape=jax.ShapeDtypeStruct((seq, d), F32),
        compiler_params=pltpu.CompilerParams(
            dimension_semantics=("parallel",), vmem_limit_bytes=VMEM_LIMIT),
        name="out_stage",
    )(y_s, y_r, proj, proj, proj, proj, x2, w_s, w_r, w_o, gain.reshape(1, d).astype(F32))


def kernel(x, norm_pre, w_in, lam_re, lam_im, log_step, s5_b_re, s5_b_im, s5_c_re, s5_c_im, s5_d,
           w_glu, b_glu, w_proj_s5, w_proj_ret, w_out, norm_post):
    bsz, seq, d = x.shape
    assert bsz == 1 and d == D_MODEL and seq % 1024 == 0
    depth = w_in.shape[0]
    x2 = x.reshape(seq, d)
    for l in range(depth):
        proj = _inproj(x2, norm_pre[l], w_in[l].astype(BF16))
        tables = _s5_tables(lam_re[l], lam_im[l], log_step[l], s5_b_re[l], s5_b_im[l],
                            s5_c_re[l], s5_c_im[l], s5_d[l], seq // S5_T)
        y_ssm = _s5core(proj, tables)
        y_s = _s5post(y_ssm, proj, w_glu[l].astype(BF16), b_glu[l])
        y_r = _retention(proj)
        x2 = _out_stage(y_s, y_r, proj, x2, w_proj_s5[l].astype(BF16), w_proj_ret[l].astype(BF16),
                        w_out[l].astype(BF16), norm_post[l])
    return x2.reshape(bsz, seq, d)
```

```python
import functools
import math

import numpy as np
import jax
import jax.numpy as jnp
from jax import lax
from jax.experimental import pallas as pl
from jax.experimental.pallas import tpu as pltpu

F32 = jnp.float32
BF16 = jnp.bfloat16
HIGHEST = lax.Precision.HIGHEST

D_MODEL = 2048
EPS = 1e-6
CHUNK = 64

S5_WIDTH = 1024
S5_GROUP = 16
S5_GROUPS = 64
S5_STATE = 64

RET_HEADS = 8
RET_V_HEAD = 128
RET_QK_HEAD = 64
RET_QK = 512
RET_WIDTH = 1024
ROPE_BASE = 10000.0

_Q_BLK512 = (2 * S5_WIDTH) // 512
_K_BLK512 = (2 * S5_WIDTH + RET_QK) // 512
_ZS_BLK = 1
_V_BLK = (2 * S5_WIDTH + 2 * RET_QK) // 1024
_ZR_BLK = _V_BLK + 1
_GLS_BLK = _ZR_BLK + 1
_GLR_BLK = _GLS_BLK + 2

VMEM_LIMIT = 56 * 1024 * 1024
SUBLANES = 8
LANES = 128

S5_T = 32
S5_GB = 8
S5_TW = S5_T * S5_GROUP
S5_RB = 1024

RET_T = 256


def _dot(a, b):
    return jnp.dot(a, b, preferred_element_type=F32)


def _rms_rows(x, gain):
    ms = jnp.mean(x * x, axis=-1, keepdims=True)
    return x * lax.rsqrt(ms + EPS) * gain


def _inproj_kernel(x_ref, g_ref, w_ref, o_ref, h_ref):
    @pl.when(pl.program_id(1) == 0)
    def _():
        h_ref[...] = _rms_rows(x_ref[...], g_ref[...]).astype(BF16)

    o_ref[...] = _dot(h_ref[...], w_ref[...]).astype(o_ref.dtype)


def _inproj(x2, gain, w_bf16, tm=1024, tn=1024):
    seq, d = x2.shape
    n = w_bf16.shape[1]
    return pl.pallas_call(
        _inproj_kernel,
        grid=(seq // tm, n // tn),
        in_specs=[
            pl.BlockSpec((tm, d), lambda i, j: (i, 0)),
            pl.BlockSpec((1, d), lambda i, j: (0, 0)),
            pl.BlockSpec((d, tn), lambda i, j: (0, j)),
        ],
        out_specs=pl.BlockSpec((tm, tn), lambda i, j: (i, j)),
        out_shape=jax.ShapeDtypeStruct((seq, n), BF16),
        scratch_shapes=[pltpu.VMEM((tm, d), BF16)],
        compiler_params=pltpu.CompilerParams(
            dimension_semantics=("parallel", "arbitrary"), vmem_limit_bytes=VMEM_LIMIT),
        name="inproj",
    )(x2, gain.reshape(1, d).astype(F32), w_bf16)


def _s5_tables(lam_re, lam_im, log_step, b_re, b_im, c_re, c_im, d_skip, n_chunks):
    t = S5_T
    lr = lam_re.astype(F32)
    li = lam_im.astype(F32)
    step = jnp.exp(log_step.astype(F32))[:, None]
    lrs = lr * step
    ang = li * step
    mag = jnp.exp(lrs)
    ab_re = mag * jnp.cos(ang)
    ab_im = mag * jnp.sin(ang)
    den = lr * lr + li * li
    nr = ab_re - 1.0
    f_re = (nr * lr + ab_im * li) / den
    f_im = (ab_im * lr - nr * li) / den
    br = b_re.astype(F32)
    bi = b_im.astype(F32)
    bb_re = f_re[..., None] * br - f_im[..., None] * bi
    bb_im = f_re[..., None] * bi + f_im[..., None] * br
    cr = c_re.astype(F32)
    ci = c_im.astype(F32)

    e_pj = ((t - 1.0) - jnp.arange(t, dtype=F32))[None, None, :]
    m_pj = jnp.exp(e_pj * lrs[:, :, None])
    qr = (m_pj * jnp.cos(e_pj * ang[:, :, None]))[..., None]
    qi = (m_pj * jnp.sin(e_pj * ang[:, :, None]))[..., None]
    ws_re = (qr * bb_re[:, :, None, :] - qi * bb_im[:, :, None, :]).reshape(S5_GROUPS, S5_STATE, S5_TW)
    ws_im = (qr * bb_im[:, :, None, :] + qi * bb_re[:, :, None, :]).reshape(S5_GROUPS, S5_STATE, S5_TW)
    w_s = jnp.concatenate([ws_re, ws_im], axis=1)

    e_ip = (jnp.arange(t, dtype=F32) + 1.0)[None, :, None]
    m_ip = jnp.exp(e_ip * lrs[:, None, :])
    pr = (m_ip * jnp.cos(e_ip * ang[:, None, :]))[:, :, None, :]
    pi = (m_ip * jnp.sin(e_ip * ang[:, None, :]))[:, :, None, :]
    co_re = (cr[:, None] * pr - ci[:, None] * pi).reshape(S5_GROUPS, S5_TW, S5_STATE)
    co_im = (cr[:, None] * pi + ci[:, None] * pr).reshape(S5_GROUPS, S5_TW, S5_STATE)
    w_o = jnp.concatenate([co_re, -co_im], axis=2).astype(BF16)

    c_cat = jnp.concatenate([cr, -ci], axis=2)
    skip = jnp.broadcast_to(d_skip.astype(F32).reshape(S5_GROUPS, S5_GROUP, 1),
                            (S5_GROUPS, S5_GROUP, LANES))

    n_steps = max(1, int(math.ceil(math.log2(n_chunks))))
    m_t = jnp.exp(float(t) * lrs)
    sq_re, sq_im = [m_t * jnp.cos(float(t) * ang)], [m_t * jnp.sin(float(t) * ang)]
    for _ in range(n_steps - 1):
        r, i = sq_re[-1], sq_im[-1]
        sq_re.append(r * r - i * i)
        sq_im.append(2.0 * r * i)
    ap_re = jnp.stack(sq_re, axis=-1)
    ap_im = jnp.stack(sq_im, axis=-1)
    return w_s, w_o, c_cat, skip, ap_re, ap_im


def _sublane_transpose(v):
    v = list(v)
    sub = lax.broadcasted_iota(jnp.int32, v[0].shape, 1)
    for k in (4, 2, 1):
        keep = (sub & k) == 0
        for r in range(SUBLANES):
            if r & k:
                continue
            a, b = v[r], v[r | k]
            v[r] = jnp.where(keep, a, pltpu.roll(b, k, 1))
            v[r | k] = jnp.where(keep, pltpu.roll(a, SUBLANES - k, 1), b)
    return v


def _s5core_kernel(n_steps, u_ref, ws_ref, wo_ref, ccat_ref, skip_ref, apr_ref, api_ref,
                   o_ref, step_scr, rhs_scr, mt_scr, yt_scr):
    t = S5_T
    nch = step_scr.shape[1]
    nq = t // SUBLANES
    nb = S5_RB // t
    nsub = nb // SUBLANES

    def deinterleave(blk, carry):
        rows = pl.ds(pl.multiple_of(blk * S5_RB, S5_RB), S5_RB)
        x = u_ref[rows, :].astype(F32).reshape(nsub, SUBLANES, nq, SUBLANES, LANES)
        dst = pl.ds(pl.multiple_of(blk * nb, nb), nb)
        for q in range(nq):
            w = _sublane_transpose([x[:, r, q] for r in range(SUBLANES)])
            for s in range(SUBLANES):
                step_scr[q * SUBLANES + s, dst, :] = w[s].reshape(nb, LANES)
        return carry

    lax.fori_loop(0, (nch * t) // S5_RB, deinterleave, 0)

    for i in range(t):
        rhs_scr[:, i * S5_GROUP:(i + 1) * S5_GROUP, :] = (
            step_scr[i].T.astype(BF16).reshape(S5_GB, S5_GROUP, nch))

    lane = lax.broadcasted_iota(jnp.int32, (S5_STATE, nch), 1)
    tail_lane = lax.broadcasted_iota(jnp.int32, (S5_GROUP, LANES), 1)
    tail_row = lax.broadcasted_iota(jnp.int32, (S5_GROUP, LANES), 0)
    lag0 = tail_lane == tail_row + (LANES - S5_GROUP)
    zero_half = jnp.zeros((S5_GROUP, S5_TW), F32)
    per_tile = LANES // S5_GROUP

    def shift_right(v, k):
        return jnp.where(lane >= k, pltpu.roll(v, k, 1), 0.0)

    def per_group(g, carry):
        ws = ws_ref[g]
        strip = jnp.dot(ccat_ref[g], ws, precision=HIGHEST, preferred_element_type=F32)
        tail = strip[:, S5_TW - LANES:] + jnp.where(lag0, skip_ref[g], 0.0)
        full = jnp.concatenate([strip[:, :S5_TW - LANES], tail, zero_half], axis=1)
        rot = [full] + [pltpu.roll(full, 2 * S5_TW - S5_GROUP * r, 1) for r in range(1, per_tile)]
        for i in range(t):
            a, r = divmod(t - 1 - i, per_tile)
            mt_scr[i * S5_GROUP:(i + 1) * S5_GROUP, :] = (
                rot[r][:, a * LANES:a * LANES + S5_TW].astype(BF16))
        rhs = rhs_scr[g]
        yt = _dot(mt_scr[...], rhs)
        s_loc = _dot(ws.astype(BF16), rhs)
        xr = s_loc[:S5_STATE]
        xi = s_loc[S5_STATE:]
        apr = apr_ref[g]
        api = api_ref[g]
        for k in range(n_steps):
            ar = apr[:, k:k + 1]
            ai = api[:, k:k + 1]
            sr = shift_right(xr, 1 << k)
            si = shift_right(xi, 1 << k)
            xr, xi = xr + (ar * sr - ai * si), xi + (ar * si + ai * sr)
        xprev = jnp.concatenate([shift_right(xr, 1), shift_right(xi, 1)], axis=0).astype(BF16)
        yt_scr[g] = yt + _dot(wo_ref[g], xprev)
        return carry

    lax.fori_loop(0, S5_GB, per_group, 0, unroll=2)

    for i in range(t):
        blk = yt_scr[:, i * S5_GROUP:(i + 1) * S5_GROUP, :].reshape(S5_GB * S5_GROUP, nch)
        step_scr[i] = blk.T

    def interleave(blk, carry):
        src = pl.ds(pl.multiple_of(blk * nb, nb), nb)
        per_q = []
        for q in range(nq):
            w = [step_scr[q * SUBLANES + s, src, :].reshape(nsub, SUBLANES, LANES)
                 for s in range(SUBLANES)]
            per_q.append(_sublane_transpose(w))
        x = jnp.stack([jnp.stack([per_q[q][r] for q in range(nq)], axis=1)
                       for r in range(SUBLANES)], axis=1)
        rows = pl.ds(pl.multiple_of(blk * S5_RB, S5_RB), S5_RB)
        o_ref[rows, :] = x.reshape(S5_RB, LANES).astype(o_ref.dtype)
        return carry

    lax.fori_loop(0, (nch * t) // S5_RB, interleave, 0)


def _s5core(proj, tables):
    seq = proj.shape[0]
    w_s, w_o, c_cat, skip, ap_re, ap_im = tables
    nch = seq // S5_T
    n_steps = ap_re.shape[-1]
    lanes = S5_GB * S5_GROUP
    grp = lambda i: (i, 0, 0)
    gspec = lambda a: pl.BlockSpec((S5_GB,) + a.shape[1:], grp)
    return pl.pallas_call(
        functools.partial(_s5core_kernel, n_steps),
        grid=(S5_GROUPS // S5_GB,),
        in_specs=[pl.BlockSpec((seq, lanes), lambda i: (0, i)),
                  gspec(w_s), gspec(w_o), gspec(c_cat), gspec(skip), gspec(ap_re), gspec(ap_im)],
        out_specs=pl.BlockSpec((seq, lanes), lambda i: (0, i)),
        out_shape=jax.ShapeDtypeStruct((seq, S5_WIDTH), BF16),
        scratch_shapes=[
            pltpu.VMEM((S5_T, nch, lanes), F32),
            pltpu.VMEM((S5_GB, S5_TW, nch), BF16),
            pltpu.VMEM((S5_TW, S5_TW), BF16),
            pltpu.VMEM((S5_GB, S5_TW, nch), F32),
        ],
        compiler_params=pltpu.CompilerParams(
            dimension_semantics=("parallel",), vmem_limit_bytes=VMEM_LIMIT),
        name="s5_core",
    )(proj, w_s, w_o, c_cat, skip, ap_re, ap_im)


def _gelu_tanh(y):
    return 0.5 * y * (1.0 + jnp.tanh(math.sqrt(2.0 / math.pi) * (y + 0.044715 * (y * y * y))))


def _s5post_kernel(y_ref, z_ref, wglu_ref, bglu_ref, o_ref):
    g = _gelu_tanh(y_ref[...].astype(F32))
    gate = jax.nn.sigmoid(_dot(g.astype(BF16), wglu_ref[...]) + bglu_ref[...])
    o_ref[...] = (g * gate * jax.nn.silu(z_ref[...].astype(F32))).astype(o_ref.dtype)


def _s5post(y, proj, w_glu_bf16, b_glu, tm=512):
    seq = y.shape[0]
    const2 = lambda i: (0, 0)
    return pl.pallas_call(
        _s5post_kernel,
        grid=(seq // tm,),
        in_specs=[
            pl.BlockSpec((tm, S5_WIDTH), lambda i: (i, 0)),
            pl.BlockSpec((tm, S5_WIDTH), lambda i: (i, _ZS_BLK)),
            pl.BlockSpec((S5_WIDTH, S5_WIDTH), const2),
            pl.BlockSpec((1, S5_WIDTH), const2),
        ],
        out_specs=pl.BlockSpec((tm, S5_WIDTH), lambda i: (i, 0)),
        out_shape=jax.ShapeDtypeStruct((seq, S5_WIDTH), BF16),
        compiler_params=pltpu.CompilerParams(
            dimension_semantics=("parallel",), vmem_limit_bytes=VMEM_LIMIT),
        name="s5_post",
    )(y, proj, w_glu_bf16, b_glu.reshape(1, S5_WIDTH).astype(F32))


def _ret_tables(seq):
    t = RET_T
    inv = ROPE_BASE ** (-jnp.arange(0, RET_QK_HEAD, 2, dtype=F32) / RET_QK_HEAD)
    tile4 = lambda a: jnp.concatenate([a, a, a, a], axis=-1)
    ang_r = jnp.arange(t, dtype=F32)[:, None] * inv[None, :]
    ang_b = (jnp.arange(seq // t, dtype=F32) * float(t))[:, None] * inv[None, :]
    cos_r, sin_r = tile4(jnp.cos(ang_r)), tile4(jnp.sin(ang_r))
    cos_b, sin_b = tile4(jnp.cos(ang_b))[:, None, :], tile4(jnp.sin(ang_b))[:, None, :]
    half = RET_QK_HEAD // 2
    sign = jnp.asarray(np.tile(np.repeat([-1.0, 1.0], half), 2)[None, :], F32)

    log_g = np.log1p(-np.exp2(-5.0 - np.arange(RET_HEADS, dtype=np.float64)))
    i = np.arange(t)
    ci = i // CHUNK
    diff = (i[:, None] - i[None, :]).astype(np.float64)
    same = ci[:, None] == ci[None, :]
    earlier = ci[None, :] < ci[:, None]
    expo = np.where(same, np.abs(diff), diff)
    dmask = np.where((same | earlier)[None], np.exp(expo[None] * log_g[:, None, None]), 0.0)
    xi = np.exp((i + 1.0)[None, :] * log_g[:, None])
    zeta = np.exp((t - 1.0 - i)[None, :] * log_g[:, None])
    xi = np.broadcast_to(xi[:, :, None], (RET_HEADS, t, RET_V_HEAD))
    zeta = np.broadcast_to(zeta[:, :, None], (RET_HEADS, t, RET_V_HEAD))
    block_decay = [float(v) for v in np.exp(t * log_g)]
    return (cos_r, sin_r, cos_b, sin_b, sign, jnp.asarray(dmask, F32), jnp.asarray(xi, F32),
            jnp.asarray(zeta, F32), block_decay)


def _swap_halves(x):
    n = x.shape[1]
    half = RET_QK_HEAD // 2
    fwd = pltpu.roll(x, half, 1)
    bwd = pltpu.roll(x, n - half, 1)
    lane = lax.broadcasted_iota(jnp.int32, x.shape, 1)
    return jnp.where((lane & (RET_QK_HEAD - 1)) < half, bwd, fwd)


def _ret_kernel(block_decay, q_ref, k_ref, v_ref, z_ref, cosr_ref, sinr_ref, cosb_ref, sinb_ref,
                sign_ref, dm_ref, xi_ref, zeta_ref, o_ref, state_ref):
    @pl.when(pl.program_id(0) == 0)
    def _():
        state_ref[...] = jnp.zeros_like(state_ref)

    cb = cosb_ref[...]
    sb = sinb_ref[...]
    cos2 = cb * cosr_ref[...] - sb * sinr_ref[...]
    sin2 = (sb * cosr_ref[...] + cb * sinr_ref[...]) * sign_ref[...]
    reps = RET_QK // cos2.shape[1]
    cos = jnp.concatenate([cos2] * reps, axis=1)
    sin = jnp.concatenate([sin2] * reps, axis=1)
    q = q_ref[...].astype(F32)
    k = k_ref[...].astype(F32)
    qr = (q * cos + _swap_halves(q) * sin).astype(BF16)
    kr = ((k * cos + _swap_halves(k) * sin) * (RET_QK_HEAD ** -0.5)).astype(BF16)
    for h in range(RET_HEADS):
        qs = slice(h * RET_QK_HEAD, (h + 1) * RET_QK_HEAD)
        vs = slice(h * RET_V_HEAD, (h + 1) * RET_V_HEAD)
        qh = qr[:, qs]
        kh = kr[:, qs]
        vh = v_ref[:, vs]
        s = lax.dot_general(qh, kh, (((1,), (1,)), ((), ())), preferred_element_type=F32)
        s = s * dm_ref[h]
        inner = _dot(s.astype(BF16), vh)
        st = state_ref[h]
        cross = _dot(qh, st.astype(BF16)) * xi_ref[h]
        vz = (vh.astype(F32) * zeta_ref[h]).astype(BF16)
        kv = lax.dot_general(kh, vz, (((0,), (0,)), ((), ())), preferred_element_type=F32)
        state_ref[h] = st * block_decay[h] + kv
        o = inner + cross
        o = o * lax.rsqrt(jnp.mean(o * o, axis=-1, keepdims=True) + EPS)
        o_ref[:, vs] = (o * jax.nn.silu(z_ref[:, vs].astype(F32))).astype(o_ref.dtype)


def _retention(proj):
    seq = proj.shape[0]
    t = RET_T
    cos_r, sin_r, cos_b, sin_b, sign, dmask, xi, zeta, block_decay = _ret_tables(seq)
    const2 = lambda i: (0, 0)
    const3 = lambda i: (0, 0, 0)
    return pl.pallas_call(
        functools.partial(_ret_kernel, block_decay),
        grid=(seq // t,),
        in_specs=[
            pl.BlockSpec((t, RET_QK), lambda i: (i, _Q_BLK512)),
            pl.BlockSpec((t, RET_QK), lambda i: (i, _K_BLK512)),
            pl.BlockSpec((t, RET_WIDTH), lambda i: (i, _V_BLK)),
            pl.BlockSpec((t, RET_WIDTH), lambda i: (i, _ZR_BLK)),
            pl.BlockSpec(cos_r.shape, const2),
            pl.BlockSpec(sin_r.shape, const2),
            pl.BlockSpec((None, 1, cos_b.shape[2]), lambda i: (i, 0, 0)),
            pl.BlockSpec((None, 1, sin_b.shape[2]), lambda i: (i, 0, 0)),
            pl.BlockSpec(sign.shape, const2),
            pl.BlockSpec(dmask.shape, const3),
            pl.BlockSpec(xi.shape, const3),
            pl.BlockSpec(zeta.shape, const3),
        ],
        out_specs=pl.BlockSpec((t, RET_WIDTH), lambda i: (i, 0)),
        out_shape=jax.ShapeDtypeStruct((seq, RET_WIDTH), BF16),
        scratch_shapes=[pltpu.VMEM((RET_HEADS, RET_QK_HEAD, RET_V_HEAD), F32)],
        compiler_params=pltpu.CompilerParams(
            dimension_semantics=("arbitrary",), vmem_limit_bytes=VMEM_LIMIT),
        name="retention",
    )(proj, proj, proj, proj, cos_r, sin_r, cos_b, sin_b, sign, dmask, xi, zeta)


def _out_kernel(ys_ref, yr_ref, gs0_ref, gs1_ref, gr0_ref, gr1_ref, x_ref, ws_ref, wr_ref, wo_ref,
                g_ref, o_ref):
    a = _dot(ys_ref[...], ws_ref[...])
    b = _dot(yr_ref[...], wr_ref[...])
    gs = jnp.concatenate([gs0_ref[...], gs1_ref[...]], axis=1).astype(F32)
    gr = jnp.concatenate([gr0_ref[...], gr1_ref[...]], axis=1).astype(F32)
    m = jax.nn.sigmoid(gs) * a + jax.nn.sigmoid(gr) * b
    o = _dot(m.astype(BF16), wo_ref[...])
    o_ref[...] = x_ref[...] + _rms_rows(o, g_ref[...])


def _out_stage(y_s, y_r, proj, x2, w_s, w_r, w_o, gain, tm=256):
    seq, d = x2.shape
    const2 = lambda i: (0, 0)
    single = pl.Buffered(1)
    blk = 1024
    return pl.pallas_call(
        _out_kernel,
        grid=(seq // tm,),
        in_specs=[
            pl.BlockSpec((tm, S5_WIDTH), lambda i: (i, 0)),
            pl.BlockSpec((tm, RET_WIDTH), lambda i: (i, 0)),
            pl.BlockSpec((tm, blk), lambda i: (i, _GLS_BLK)),
            pl.BlockSpec((tm, blk), lambda i: (i, _GLS_BLK + 1)),
            pl.BlockSpec((tm, blk), lambda i: (i, _GLR_BLK)),
            pl.BlockSpec((tm, blk), lambda i: (i, _GLR_BLK + 1)),
            pl.BlockSpec((tm, d), lambda i: (i, 0)),
            pl.BlockSpec(w_s.shape, const2, pipeline_mode=single),
            pl.BlockSpec(w_r.shape, const2, pipeline_mode=single),
            pl.BlockSpec(w_o.shape, const2, pipeline_mode=single),
            pl.BlockSpec((1, d), const2),
        ],
        out_specs=pl.BlockSpec((tm, d), lambda i: (i, 0)),
        out_shape=jax.ShapeDtypeStruct((seq, d), F32),
        compiler_params=pltpu.CompilerParams(
            dimension_semantics=("parallel",), vmem_limit_bytes=VMEM_LIMIT),
        name="out_stage",
    )(y_s, y_r, proj, proj, proj, proj, x2, w_s, w_r, w_o, gain.reshape(1, d).astype(F32))


def kernel(x, norm_pre, w_in, lam_re, lam_im, log_step, s5_b_re, s5_b_im, s5_c_re, s5_c_im, s5_d,
           w_glu, b_glu, w_proj_s5, w_proj_ret, w_out, norm_post):
    bsz, seq, d = x.shape
    assert bsz == 1 and d == D_MODEL and seq % 1024 == 0
    depth = w_in.shape[0]
    x2 = x.reshape(seq, d)
    for l in range(depth):
        proj = _inproj(x2, norm_pre[l], w_in[l].astype(BF16))
        tables = _s5_tables(lam_re[l], lam_im[l], log_step[l], s5_b_re[l], s5_b_im[l],
                            s5_c_re[l], s5_c_im[l], s5_d[l], seq // S5_T)
        y_ssm = _s5core(proj, tables)
        y_s = _s5post(y_ssm, proj, w_glu[l].astype(BF16), b_glu[l])
        y_r = _retention(proj)
        x2 = _out_stage(y_s, y_r, proj, x2, w_proj_s5[l].astype(BF16), w_proj_ret[l].astype(BF16),
                        w_out[l].astype(BF16), norm_post[l])
    return x2.reshape(bsz, seq, d)
```

```python
import functools
import math

import numpy as np
import jax
import jax.numpy as jnp
from jax import lax
from jax.experimental import pallas as pl
from jax.experimental.pallas import tpu as pltpu

F32 = jnp.float32
BF16 = jnp.bfloat16
HIGHEST = lax.Precision.HIGHEST

D_MODEL = 2048
EPS = 1e-6
CHUNK = 64

S5_WIDTH = 1024
S5_GROUP = 16
S5_GROUPS = 64
S5_STATE = 64

RET_HEADS = 8
RET_V_HEAD = 128
RET_QK_HEAD = 64
RET_QK = 512
RET_WIDTH = 1024
ROPE_BASE = 10000.0

_Q_BLK512 = (2 * S5_WIDTH) // 512
_K_BLK512 = (2 * S5_WIDTH + RET_QK) // 512
_ZS_BLK = 1
_V_BLK = (2 * S5_WIDTH + 2 * RET_QK) // 1024
_ZR_BLK = _V_BLK + 1
_GLS_BLK = _ZR_BLK + 1
_GLR_BLK = _GLS_BLK + 2

VMEM_LIMIT = 56 * 1024 * 1024
SUBLANES = 8
LANES = 128

S5_T = 32
S5_GB = 8
S5_TW = S5_T * S5_GROUP
S5_RB = 1024

RET_T = 256


def _dot(a, b):
    return jnp.dot(a, b, preferred_element_type=F32)


def _rms_rows(x, gain):
    ms = jnp.mean(x * x, axis=-1, keepdims=True)
    return x * lax.rsqrt(ms + EPS) * gain


def _inproj_kernel(x_ref, g_ref, w_ref, o_ref, h_ref):
    @pl.when(pl.program_id(1) == 0)
    def _():
        h_ref[...] = _rms_rows(x_ref[...], g_ref[...]).astype(BF16)

    o_ref[...] = _dot(h_ref[...], w_ref[...]).astype(o_ref.dtype)


def _inproj(x2, gain, w_bf16, tm=1024, tn=1024):
    seq, d = x2.shape
    n = w_bf16.shape[1]
    return pl.pallas_call(
        _inproj_kernel,
        grid=(seq // tm, n // tn),
        in_specs=[
            pl.BlockSpec((tm, d), lambda i, j: (i, 0)),
            pl.BlockSpec((1, d), lambda i, j: (0, 0)),
            pl.BlockSpec((d, tn), lambda i, j: (0, j)),
        ],
        out_specs=pl.BlockSpec((tm, tn), lambda i, j: (i, j)),
        out_shape=jax.ShapeDtypeStruct((seq, n), BF16),
        scratch_shapes=[pltpu.VMEM((tm, d), BF16)],
        compiler_params=pltpu.CompilerParams(
            dimension_semantics=("parallel", "arbitrary"), vmem_limit_bytes=VMEM_LIMIT),
        name="inproj",
    )(x2, gain.reshape(1, d).astype(F32), w_bf16)


def _s5_tables(lam_re, lam_im, log_step, b_re, b_im, c_re, c_im, d_skip, n_chunks):
    t = S5_T
    lr = lam_re.astype(F32)
    li = lam_im.astype(F32)
    step = jnp.exp(log_step.astype(F32))[:, None]
    lrs = lr * step
    ang = li * step
    mag = jnp.exp(lrs)
    ab_re = mag * jnp.cos(ang)
    ab_im = mag * jnp.sin(ang)
    den = lr * lr + li * li
    nr = ab_re - 1.0
    f_re = (nr * lr + ab_im * li) / den
    f_im = (ab_im * lr - nr * li) / den
    br = b_re.astype(F32)
    bi = b_im.astype(F32)
    bb_re = f_re[..., None] * br - f_im[..., None] * bi
    bb_im = f_re[..., None] * bi + f_im[..., None] * br
    cr = c_re.astype(F32)
    ci = c_im.astype(F32)

    e_pj = ((t - 1.0) - jnp.arange(t, dtype=F32))[None, None, :]
    m_pj = jnp.exp(e_pj * lrs[:, :, None])
    qr = (m_pj * jnp.cos(e_pj * ang[:, :, None]))[..., None]
    qi = (m_pj * jnp.sin(e_pj * ang[:, :, None]))[..., None]
    ws_re = (qr * bb_re[:, :, None, :] - qi * bb_im[:, :, None, :]).reshape(S5_GROUPS, S5_STATE, S5_TW)
    ws_im = (qr * bb_im[:, :, None, :] + qi * bb_re[:, :, None, :]).reshape(S5_GROUPS, S5_STATE, S5_TW)
    w_s = jnp.concatenate([ws_re, ws_im], axis=1)

    e_ip = (jnp.arange(t, dtype=F32) + 1.0)[None, :, None]
    m_ip = jnp.exp(e_ip * lrs[:, None, :])
    pr = (m_ip * jnp.cos(e_ip * ang[:, None, :]))[:, :, None, :]
    pi = (m_ip * jnp.sin(e_ip * ang[:, None, :]))[:, :, None, :]
    co_re = (cr[:, None] * pr - ci[:, None] * pi).reshape(S5_GROUPS, S5_TW, S5_STATE)
    co_im = (cr[:, None] * pi + ci[:, None] * pr).reshape(S5_GROUPS, S5_TW, S5_STATE)
    w_o = jnp.concatenate([co_re, -co_im], axis=2).astype(BF16)

    c_cat = jnp.concatenate([cr, -ci], axis=2)
    skip = jnp.broadcast_to(d_skip.astype(F32).reshape(S5_GROUPS, S5_GROUP, 1),
                            (S5_GROUPS, S5_GROUP, LANES))

    n_steps = max(1, int(math.ceil(math.log2(n_chunks))))
    m_t = jnp.exp(float(t) * lrs)
    sq_re, sq_im = [m_t * jnp.cos(float(t) * ang)], [m_t * jnp.sin(float(t) * ang)]
    for _ in range(n_steps - 1):
        r, i = sq_re[-1], sq_im[-1]
        sq_re.append(r * r - i * i)
        sq_im.append(2.0 * r * i)
    ap_re = jnp.stack(sq_re, axis=-1)
    ap_im = jnp.stack(sq_im, axis=-1)
    return w_s, w_o, c_cat, skip, ap_re, ap_im


def _sublane_transpose(v):
    v = list(v)
    sub = lax.broadcasted_iota(jnp.int32, v[0].shape, 1)
    for k in (4, 2, 1):
        keep = (sub & k) == 0
        for r in range(SUBLANES):
            if r & k:
                continue
            a, b = v[r], v[r | k]
            v[r] = jnp.where(keep, a, pltpu.roll(b, k, 1))
            v[r | k] = jnp.where(keep, pltpu.roll(a, SUBLANES - k, 1), b)
    return v


def _s5core_kernel(n_steps, u_ref, ws_ref, wo_ref, ccat_ref, skip_ref, apr_ref, api_ref,
                   o_ref, step_scr, rhs_scr, mt_scr, yt_scr):
    t = S5_T
    nch = step_scr.shape[1]
    nq = t // SUBLANES
    nb = S5_RB // t
    nsub = nb // SUBLANES

    def deinterleave(blk, carry):
        rows = pl.ds(pl.multiple_of(blk * S5_RB, S5_RB), S5_RB)
        x = u_ref[rows, :].astype(F32).reshape(nsub, SUBLANES, nq, SUBLANES, LANES)
        dst = pl.ds(pl.multiple_of(blk * nb, nb), nb)
        for q in range(nq):
            w = _sublane_transpose([x[:, r, q] for r in range(SUBLANES)])
            for s in range(SUBLANES):
                step_scr[q * SUBLANES + s, dst, :] = w[s].reshape(nb, LANES)
        return carry

    lax.fori_loop(0, (nch * t) // S5_RB, deinterleave, 0)

    for i in range(t):
        rhs_scr[:, i * S5_GROUP:(i + 1) * S5_GROUP, :] = (
            step_scr[i].T.astype(BF16).reshape(S5_GB, S5_GROUP, nch))

    lane = lax.broadcasted_iota(jnp.int32, (S5_STATE, nch), 1)
    tail_lane = lax.broadcasted_iota(jnp.int32, (S5_GROUP, LANES), 1)
    tail_row = lax.broadcasted_iota(jnp.int32, (S5_GROUP, LANES), 0)
    lag0 = tail_lane == tail_row + (LANES - S5_GROUP)
    zero_half = jnp.zeros((S5_GROUP, S5_TW), F32)
    per_tile = LANES // S5_GROUP

    def shift_right(v, k):
        return jnp.where(lane >= k, pltpu.roll(v, k, 1), 0.0)

    def per_group(g, carry):
        ws = ws_ref[g]
        strip = jnp.dot(ccat_ref[g], ws, precision=HIGHEST, preferred_element_type=F32)
        tail = strip[:, S5_TW - LANES:] + jnp.where(lag0, skip_ref[g], 0.0)
        full = jnp.concatenate([strip[:, :S5_TW - LANES], tail, zero_half], axis=1)
        rot = [full] + [pltpu.roll(full, 2 * S5_TW - S5_GROUP * r, 1) for r in range(1, per_tile)]
        for i in range(t):
            a, r = divmod(t - 1 - i, per_tile)
            mt_scr[i * S5_GROUP:(i + 1) * S5_GROUP, :] = (
                rot[r][:, a * LANES:a * LANES + S5_TW].astype(BF16))
        rhs = rhs_scr[g]
        yt = _dot(mt_scr[...], rhs)
        s_loc = _dot(ws.astype(BF16), rhs)
        xr = s_loc[:S5_STATE]
        xi = s_loc[S5_STATE:]
        apr = apr_ref[g]
        api = api_ref[g]
        for k in range(n_steps):
            ar = apr[:, k:k + 1]
            ai = api[:, k:k + 1]
            sr = shift_right(xr, 1 << k)
            si = shift_right(xi, 1 << k)
            xr, xi = xr + (ar * sr - ai * si), xi + (ar * si + ai * sr)
        xprev = jnp.concatenate([shift_right(xr, 1), shift_right(xi, 1)], axis=0).astype(BF16)
        yt_scr[g] = yt + _dot(wo_ref[g], xprev)
        return carry

    lax.fori_loop(0, S5_GB, per_group, 0, unroll=2)

    for i in range(t):
        blk = yt_scr[:, i * S5_GROUP:(i + 1) * S5_GROUP, :].reshape(S5_GB * S5_GROUP, nch)
        step_scr[i] = blk.T

    def interleave(blk, carry):
        src = pl.ds(pl.multiple_of(blk * nb, nb), nb)
        per_q = []
        for q in range(nq):
            w = [step_scr[q * SUBLANES + s, src, :].reshape(nsub, SUBLANES, LANES)
                 for s in range(SUBLANES)]
            per_q.append(_sublane_transpose(w))
        x = jnp.stack([jnp.stack([per_q[q][r] for q in range(nq)], axis=1)
                       for r in range(SUBLANES)], axis=1)
        rows = pl.ds(pl.multiple_of(blk * S5_RB, S5_RB), S5_RB)
        o_ref[rows, :] = x.reshape(S5_RB, LANES).astype(o_ref.dtype)
        return carry

    lax.fori_loop(0, (nch * t) // S5_RB, interleave, 0)


def _s5core(proj, tables):
    seq = proj.shape[0]
    w_s, w_o, c_cat, skip, ap_re, ap_im = tables
    nch = seq // S5_T
    n_steps = ap_re.shape[-1]
    lanes = S5_GB * S5_GROUP
    grp = lambda i: (i, 0, 0)
    gspec = lambda a: pl.BlockSpec((S5_GB,) + a.shape[1:], grp)
    return pl.pallas_call(
        functools.partial(_s5core_kernel, n_steps),
        grid=(S5_GROUPS // S5_GB,),
        in_specs=[pl.BlockSpec((seq, lanes), lambda i: (0, i)),
                  gspec(w_s), gspec(w_o), gspec(c_cat), gspec(skip), gspec(ap_re), gspec(ap_im)],
        out_specs=pl.BlockSpec((seq, lanes), lambda i: (0, i)),
        out_shape=jax.ShapeDtypeStruct((seq, S5_WIDTH), BF16),
        scratch_shapes=[
            pltpu.VMEM((S5_T, nch, lanes), F32),
            pltpu.VMEM((S5_GB, S5_TW, nch), BF16),
            pltpu.VMEM((S5_TW, S5_TW), BF16),
            pltpu.VMEM((S5_GB, S5_TW, nch), F32),
        ],
        compiler_params=pltpu.CompilerParams(
            dimension_semantics=("parallel",), vmem_limit_bytes=VMEM_LIMIT),
        name="s5_core",
    )(proj, w_s, w_o, c_cat, skip, ap_re, ap_im)


def _gelu_tanh(y):
    return 0.5 * y * (1.0 + jnp.tanh(math.sqrt(2.0 / math.pi) * (y + 0.044715 * (y * y * y))))


def _s5post_block(y_ref, z_ref, wglu_ref, bglu_ref, o_ref):
    g = _gelu_tanh(y_ref[...].astype(F32))
    gate = jax.nn.sigmoid(_dot(g.astype(BF16), wglu_ref[...]) + bglu_ref[...])
    o_ref[...] = (g * gate * jax.nn.silu(z_ref[...].astype(F32))).astype(o_ref.dtype)


def _ret_tables(seq):
    t = RET_T
    inv = ROPE_BASE ** (-jnp.arange(0, RET_QK_HEAD, 2, dtype=F32) / RET_QK_HEAD)
    tile4 = lambda a: jnp.concatenate([a, a, a, a], axis=-1)
    ang_r = jnp.arange(t, dtype=F32)[:, None] * inv[None, :]
    ang_b = (jnp.arange(seq // t, dtype=F32) * float(t))[:, None] * inv[None, :]
    cos_r, sin_r = tile4(jnp.cos(ang_r)), tile4(jnp.sin(ang_r))
    cos_b, sin_b = tile4(jnp.cos(ang_b))[:, None, :], tile4(jnp.sin(ang_b))[:, None, :]
    half = RET_QK_HEAD // 2
    sign = jnp.asarray(np.tile(np.repeat([-1.0, 1.0], half), 2)[None, :], F32)

    log_g = np.log1p(-np.exp2(-5.0 - np.arange(RET_HEADS, dtype=np.float64)))
    i = np.arange(t)
    ci = i // CHUNK
    diff = (i[:, None] - i[None, :]).astype(np.float64)
    same = ci[:, None] == ci[None, :]
    earlier = ci[None, :] < ci[:, None]
    expo = np.where(same, np.abs(diff), diff)
    dmask = np.where((same | earlier)[None], np.exp(expo[None] * log_g[:, None, None]), 0.0)
    xi = np.exp((i + 1.0)[None, :] * log_g[:, None])
    zeta = np.exp((t - 1.0 - i)[None, :] * log_g[:, None])
    xi = np.broadcast_to(xi[:, :, None], (RET_HEADS, t, RET_V_HEAD))
    zeta = np.broadcast_to(zeta[:, :, None], (RET_HEADS, t, RET_V_HEAD))
    block_decay = [float(v) for v in np.exp(t * log_g)]
    return (cos_r, sin_r, cos_b, sin_b, sign, jnp.asarray(dmask, F32), jnp.asarray(xi, F32),
            jnp.asarray(zeta, F32), block_decay)


def _swap_halves(x):
    n = x.shape[1]
    half = RET_QK_HEAD // 2
    fwd = pltpu.roll(x, half, 1)
    bwd = pltpu.roll(x, n - half, 1)
    lane = lax.broadcasted_iota(jnp.int32, x.shape, 1)
    return jnp.where((lane & (RET_QK_HEAD - 1)) < half, bwd, fwd)


def _retention_block(block_decay, q_ref, k_ref, v_ref, z_ref, cosr_ref, sinr_ref, cosb_ref, sinb_ref,
                     sign_ref, dm_ref, xi_ref, zeta_ref, o_ref, state_ref):
    cb = cosb_ref[...]
    sb = sinb_ref[...]
    cos2 = cb * cosr_ref[...] - sb * sinr_ref[...]
    sin2 = (sb * cosr_ref[...] + cb * sinr_ref[...]) * sign_ref[...]
    reps = RET_QK // cos2.shape[1]
    cos = jnp.concatenate([cos2] * reps, axis=1)
    sin = jnp.concatenate([sin2] * reps, axis=1)
    q = q_ref[...].astype(F32)
    k = k_ref[...].astype(F32)
    qr = (q * cos + _swap_halves(q) * sin).astype(BF16)
    kr = ((k * cos + _swap_halves(k) * sin) * (RET_QK_HEAD ** -0.5)).astype(BF16)
    for h in range(RET_HEADS):
        qs = slice(h * RET_QK_HEAD, (h + 1) * RET_QK_HEAD)
        vs = slice(h * RET_V_HEAD, (h + 1) * RET_V_HEAD)
        qh = qr[:, qs]
        kh = kr[:, qs]
        vh = v_ref[:, vs]
        s = lax.dot_general(qh, kh, (((1,), (1,)), ((), ())), preferred_element_type=F32)
        s = s * dm_ref[h]
        inner = _dot(s.astype(BF16), vh)
        st = state_ref[h]
        cross = _dot(qh, st.astype(BF16)) * xi_ref[h]
        vz = (vh.astype(F32) * zeta_ref[h]).astype(BF16)
        kv = lax.dot_general(kh, vz, (((0,), (0,)), ((), ())), preferred_element_type=F32)
        state_ref[h] = st * block_decay[h] + kv
        o = inner + cross
        o = o * lax.rsqrt(jnp.mean(o * o, axis=-1, keepdims=True) + EPS)
        o_ref[:, vs] = (o * jax.nn.silu(z_ref[:, vs].astype(F32))).astype(o_ref.dtype)


def _merge_block(ys_ref, yr_ref, gs0_ref, gs1_ref, gr0_ref, gr1_ref, x_ref, ws_ref, wr_ref, wo_ref,
                 g_ref, o_ref):
    a = _dot(ys_ref[...], ws_ref[...])
    b = _dot(yr_ref[...], wr_ref[...])
    gs = jnp.concatenate([gs0_ref[...], gs1_ref[...]], axis=1).astype(F32)
    gr = jnp.concatenate([gr0_ref[...], gr1_ref[...]], axis=1).astype(F32)
    m = jax.nn.sigmoid(gs) * a + jax.nn.sigmoid(gr) * b
    o = _dot(m.astype(BF16), wo_ref[...])
    o_ref[...] = x_ref[...] + _rms_rows(o, g_ref[...])


def _tail_kernel(block_decay,
                 yssm_ref, zs_ref, q_ref, k_ref, v_ref, zr_ref, cosb_ref, sinb_ref,
                 gs0_ref, gs1_ref, gr0_ref, gr1_ref, x_ref,
                 wglu_ref, bglu_ref, cosr_ref, sinr_ref, sign_ref, dm_ref, xi_ref, zeta_ref,
                 ws_ref, wr_ref, wo_ref, g_ref,
                 o_ref, state_ref, ys_scr, yr_scr):
    @pl.when(pl.program_id(0) == 0)
    def _():
        state_ref[...] = jnp.zeros_like(state_ref)
        ys_scr[...] = jnp.zeros_like(ys_scr)
        yr_scr[...] = jnp.zeros_like(yr_scr)

    _merge_block(ys_scr, yr_scr, gs0_ref, gs1_ref, gr0_ref, gr1_ref, x_ref, ws_ref, wr_ref, wo_ref,
                 g_ref, o_ref)
    _s5post_block(yssm_ref, zs_ref, wglu_ref, bglu_ref, ys_scr)
    _retention_block(block_decay, q_ref, k_ref, v_ref, zr_ref, cosr_ref, sinr_ref, cosb_ref, sinb_ref,
                     sign_ref, dm_ref, xi_ref, zeta_ref, yr_scr, state_ref)


def _tail(y_ssm, proj, x2, w_glu, b_glu, w_s, w_r, w_o, gain):
    seq, d = x2.shape
    t = RET_T
    nblk = seq // t
    cos_r, sin_r, cos_b, sin_b, sign, dmask, xi, zeta, block_decay = _ret_tables(seq)
    blk = 1024
    cur = lambda s: jnp.minimum(s, nblk - 1)
    prv = lambda s: jnp.maximum(s - 1, 0)
    single = pl.Buffered(1)
    const = lambda a: pl.BlockSpec(a.shape, lambda s: (0,) * a.ndim, pipeline_mode=single)
    b_glu2 = b_glu.reshape(1, S5_WIDTH).astype(F32)
    gain2 = gain.reshape(1, d).astype(F32)
    return pl.pallas_call(
        functools.partial(_tail_kernel, block_decay),
        grid=(nblk + 1,),
        in_specs=[
            pl.BlockSpec((t, S5_WIDTH), lambda s: (cur(s), 0)),
            pl.BlockSpec((t, S5_WIDTH), lambda s: (cur(s), _ZS_BLK)),
            pl.BlockSpec((t, RET_QK), lambda s: (cur(s), _Q_BLK512)),
            pl.BlockSpec((t, RET_QK), lambda s: (cur(s), _K_BLK512)),
            pl.BlockSpec((t, RET_WIDTH), lambda s: (cur(s), _V_BLK)),
            pl.BlockSpec((t, RET_WIDTH), lambda s: (cur(s), _ZR_BLK)),
            pl.BlockSpec((None, 1, cos_b.shape[2]), lambda s: (cur(s), 0, 0)),
            pl.BlockSpec((None, 1, sin_b.shape[2]), lambda s: (cur(s), 0, 0)),
            pl.BlockSpec((t, blk), lambda s: (prv(s), _GLS_BLK)),
            pl.BlockSpec((t, blk), lambda s: (prv(s), _GLS_BLK + 1)),
            pl.BlockSpec((t, blk), lambda s: (prv(s), _GLR_BLK)),
            pl.BlockSpec((t, blk), lambda s: (prv(s), _GLR_BLK + 1)),
            pl.BlockSpec((t, d), lambda s: (prv(s), 0)),
            const(w_glu), const(b_glu2), const(cos_r), const(sin_r), const(sign), const(dmask),
            const(xi), const(zeta), const(w_s), const(w_r), const(w_o), const(gain2),
        ],
        out_specs=pl.BlockSpec((t, d), lambda s: (prv(s), 0)),
        out_shape=jax.ShapeDtypeStruct((seq, d), F32),
        scratch_shapes=[
            pltpu.VMEM((RET_HEADS, RET_QK_HEAD, RET_V_HEAD), F32),
            pltpu.VMEM((t, S5_WIDTH), BF16),
            pltpu.VMEM((t, RET_WIDTH), BF16),
        ],
        compiler_params=pltpu.CompilerParams(
            dimension_semantics=("arbitrary",), vmem_limit_bytes=VMEM_LIMIT),
        name="tail",
    )(y_ssm, proj, proj, proj, proj, proj, cos_b, sin_b, proj, proj, proj, proj, x2,
      w_glu, b_glu2, cos_r, sin_r, sign, dmask, xi, zeta, w_s, w_r, w_o, gain2)


def kernel(x, norm_pre, w_in, lam_re, lam_im, log_step, s5_b_re, s5_b_im, s5_c_re, s5_c_im, s5_d,
           w_glu, b_glu, w_proj_s5, w_proj_ret, w_out, norm_post):
    bsz, seq, d = x.shape
    assert bsz == 1 and d == D_MODEL and seq % 1024 == 0
    depth = w_in.shape[0]
    x2 = x.reshape(seq, d)
    for l in range(depth):
        proj = _inproj(x2, norm_pre[l], w_in[l].astype(BF16))
        tables = _s5_tables(lam_re[l], lam_im[l], log_step[l], s5_b_re[l], s5_b_im[l],
                            s5_c_re[l], s5_c_im[l], s5_d[l], seq // S5_T)
        y_ssm = _s5core(proj, tables)
        x2 = _tail(y_ssm, proj, x2, w_glu[l].astype(BF16), b_glu[l], w_proj_s5[l].astype(BF16),
                   w_proj_ret[l].astype(BF16), w_out[l].astype(BF16), norm_post[l])
    return x2.reshape(bsz, seq, d)
```

```python
import functools
import math

import numpy as np
import jax
import jax.numpy as jnp
from jax import lax
from jax.experimental import pallas as pl
from jax.experimental.pallas import tpu as pltpu

F32 = jnp.float32
BF16 = jnp.bfloat16
HIGHEST = lax.Precision.HIGHEST

D_MODEL = 2048
EPS = 1e-6
CHUNK = 64

S5_WIDTH = 1024
S5_GROUP = 16
S5_GROUPS = 64
S5_STATE = 64

RET_HEADS = 8
RET_V_HEAD = 128
RET_QK_HEAD = 64
RET_QK = 512
RET_WIDTH = 1024
ROPE_BASE = 10000.0

_Q_BLK512 = (2 * S5_WIDTH) // 512
_K_BLK512 = (2 * S5_WIDTH + RET_QK) // 512
_ZS_BLK = 1
_V_BLK = (2 * S5_WIDTH + 2 * RET_QK) // 1024
_ZR_BLK = _V_BLK + 1
_GLS_BLK = _ZR_BLK + 1
_GLR_BLK = _GLS_BLK + 2

VMEM_LIMIT = 56 * 1024 * 1024
SUBLANES = 8
LANES = 128

S5_T = 32
S5_GB = 8
S5_TW = S5_T * S5_GROUP
S5_RB = 1024
S5_SCAN_ROWS = 16
S5_BATCH = 4

RET_T = 256


def _dot(a, b):
    return jnp.dot(a, b, preferred_element_type=F32)


def _rms_rows(x, gain):
    ms = jnp.mean(x * x, axis=-1, keepdims=True)
    return x * lax.rsqrt(ms + EPS) * gain


def _inproj_kernel(x_ref, g_ref, w_ref, o_ref, h_ref):
    @pl.when(pl.program_id(1) == 0)
    def _():
        h_ref[...] = _rms_rows(x_ref[...], g_ref[...]).astype(BF16)

    o_ref[...] = _dot(h_ref[...], w_ref[...]).astype(o_ref.dtype)


def _inproj(x2, gain, w_bf16, tm=1024, tn=1536):
    seq, d = x2.shape
    n = w_bf16.shape[1]
    return pl.pallas_call(
        _inproj_kernel,
        grid=(seq // tm, n // tn),
        in_specs=[
            pl.BlockSpec((tm, d), lambda i, j: (i, 0)),
            pl.BlockSpec((1, d), lambda i, j: (0, 0)),
            pl.BlockSpec((d, tn), lambda i, j: (0, j)),
        ],
        out_specs=pl.BlockSpec((tm, tn), lambda i, j: (i, j)),
        out_shape=jax.ShapeDtypeStruct((seq, n), BF16),
        scratch_shapes=[pltpu.VMEM((tm, d), BF16)],
        compiler_params=pltpu.CompilerParams(
            dimension_semantics=("parallel", "arbitrary"), vmem_limit_bytes=VMEM_LIMIT),
        name="inproj",
    )(x2, gain.reshape(1, d).astype(F32), w_bf16)


def _s5_tables(lam_re, lam_im, log_step, b_re, b_im, c_re, c_im, d_skip, n_chunks):
    t = S5_T
    lr = lam_re.astype(F32)
    li = lam_im.astype(F32)
    step = jnp.exp(log_step.astype(F32))[:, None]
    lrs = lr * step
    ang = li * step
    mag = jnp.exp(lrs)
    ab_re = mag * jnp.cos(ang)
    ab_im = mag * jnp.sin(ang)
    den = lr * lr + li * li
    nr = ab_re - 1.0
    f_re = (nr * lr + ab_im * li) / den
    f_im = (ab_im * lr - nr * li) / den
    br = b_re.astype(F32)
    bi = b_im.astype(F32)
    bb_re = f_re[..., None] * br - f_im[..., None] * bi
    bb_im = f_re[..., None] * bi + f_im[..., None] * br
    cr = c_re.astype(F32)
    ci = c_im.astype(F32)

    e_pj = ((t - 1.0) - jnp.arange(t, dtype=F32))[None, None, :]
    m_pj = jnp.exp(e_pj * lrs[:, :, None])
    qr = (m_pj * jnp.cos(e_pj * ang[:, :, None]))[..., None]
    qi = (m_pj * jnp.sin(e_pj * ang[:, :, None]))[..., None]
    ws_re = (qr * bb_re[:, :, None, :] - qi * bb_im[:, :, None, :]).reshape(S5_GROUPS, S5_STATE, S5_TW)
    ws_im = (qr * bb_im[:, :, None, :] + qi * bb_re[:, :, None, :]).reshape(S5_GROUPS, S5_STATE, S5_TW)
    w_s = jnp.concatenate([ws_re, ws_im], axis=1)

    e_ip = (jnp.arange(t, dtype=F32) + 1.0)[None, :, None]
    m_ip = jnp.exp(e_ip * lrs[:, None, :])
    pr = (m_ip * jnp.cos(e_ip * ang[:, None, :]))[:, :, None, :]
    pi = (m_ip * jnp.sin(e_ip * ang[:, None, :]))[:, :, None, :]
    co_re = (cr[:, None] * pr - ci[:, None] * pi).reshape(S5_GROUPS, S5_TW, S5_STATE)
    co_im = (cr[:, None] * pi + ci[:, None] * pr).reshape(S5_GROUPS, S5_TW, S5_STATE)
    w_o = jnp.concatenate([co_re, -co_im], axis=2).astype(BF16)

    c_cat = jnp.concatenate([cr, -ci], axis=2)
    skip = jnp.broadcast_to(d_skip.astype(F32).reshape(S5_GROUPS, S5_GROUP, 1),
                            (S5_GROUPS, S5_GROUP, LANES))

    n_steps = max(1, int(math.ceil(math.log2(n_chunks))))
    m_t = jnp.exp(float(t) * lrs)
    sq_re, sq_im = [m_t * jnp.cos(float(t) * ang)], [m_t * jnp.sin(float(t) * ang)]
    for _ in range(n_steps - 1):
        r, i = sq_re[-1], sq_im[-1]
        sq_re.append(r * r - i * i)
        sq_im.append(2.0 * r * i)
    def pair_rows(sq):
        a = jnp.stack(sq, axis=0).reshape(n_steps, S5_GROUPS // 2, 2 * S5_STATE)
        a = jnp.transpose(a, (1, 0, 2))
        return jnp.pad(a, ((0, 0), (0, S5_SCAN_ROWS - n_steps), (0, 0)))

    return w_s, w_o, c_cat, skip, pair_rows(sq_re), pair_rows(sq_im), n_steps


def _sublane_transpose(v):
    v = list(v)
    sub = lax.broadcasted_iota(jnp.int32, v[0].shape, 1)
    for k in (4, 2, 1):
        keep = (sub & k) == 0
        for r in range(SUBLANES):
            if r & k:
                continue
            a, b = v[r], v[r | k]
            v[r] = jnp.where(keep, a, pltpu.roll(b, k, 1))
            v[r | k] = jnp.where(keep, pltpu.roll(a, SUBLANES - k, 1), b)
    return v


def _s5core_kernel(n_steps, u_ref, ws_ref, wo_ref, ccat_ref, skip_ref, apr_ref, api_ref,
                   o_ref, step_scr, rhs_scr, mt_scr, yt_scr):
    t = S5_T
    nch = step_scr.shape[1]
    nq = t // SUBLANES
    nb = S5_RB // t
    nsub = nb // SUBLANES

    def deinterleave(blk, carry):
        rows = pl.ds(pl.multiple_of(blk * S5_RB, S5_RB), S5_RB)
        x = u_ref[rows, :].astype(F32).reshape(nsub, SUBLANES, nq, SUBLANES, LANES)
        dst = pl.ds(pl.multiple_of(blk * nb, nb), nb)
        for q in range(nq):
            w = _sublane_transpose([x[:, r, q] for r in range(SUBLANES)])
            for s in range(SUBLANES):
                step_scr[q * SUBLANES + s, dst, :] = w[s].reshape(nb, LANES)
        return carry

    lax.fori_loop(0, (nch * t) // S5_RB, deinterleave, 0)

    for i in range(t):
        rhs_scr[:, i * S5_GROUP:(i + 1) * S5_GROUP, :] = (
            step_scr[i].T.astype(BF16).reshape(S5_GB, S5_GROUP, nch))

    tail_lane = lax.broadcasted_iota(jnp.int32, (S5_GROUP, LANES), 1)
    tail_row = lax.broadcasted_iota(jnp.int32, (S5_GROUP, LANES), 0)
    lag0 = tail_lane == tail_row + (LANES - S5_GROUP)
    zero_half = jnp.zeros((S5_GROUP, S5_TW), F32)
    per_tile = LANES // S5_GROUP
    chunk_row = lax.broadcasted_iota(jnp.int32, (nch, LANES), 0)

    def shift_down(v, k):
        if k % SUBLANES == 0:
            return jnp.concatenate([jnp.zeros((k, LANES), F32), v[:nch - k]], axis=0)
        return jnp.where(chunk_row >= k, pltpu.roll(v, k, 0), 0.0)

    def local_products(g, slot):
        mt = mt_scr.at[slot]
        ws = ws_ref[g]
        strip = jnp.dot(ccat_ref[g], ws, precision=HIGHEST, preferred_element_type=F32)
        tail = strip[:, S5_TW - LANES:] + jnp.where(lag0, skip_ref[g], 0.0)
        full = jnp.concatenate([strip[:, :S5_TW - LANES], tail, zero_half], axis=1)
        rot = [full] + [pltpu.roll(full, 2 * S5_TW - S5_GROUP * r, 1) for r in range(1, per_tile)]
        for i in range(t):
            a, r = divmod(t - 1 - i, per_tile)
            mt[i * S5_GROUP:(i + 1) * S5_GROUP, :] = (
                rot[r][:, a * LANES:a * LANES + S5_TW].astype(BF16))
        rhs = rhs_scr[g]
        yt_scr[g] = _dot(mt[...], rhs)
        return _dot(ws.astype(BF16), rhs)

    def scan_pair(gp, g0, s0, s1):
        xr = jnp.concatenate([s0[:S5_STATE], s1[:S5_STATE]], axis=0).T
        xi = jnp.concatenate([s0[S5_STATE:], s1[S5_STATE:]], axis=0).T
        apr = apr_ref[gp]
        api = api_ref[gp]
        for k in range(n_steps):
            ar = apr[k:k + 1, :]
            ai = api[k:k + 1, :]
            sr = shift_down(xr, 1 << k)
            si = shift_down(xi, 1 << k)
            xr, xi = xr + (ar * sr - ai * si), xi + (ar * si + ai * sr)
        pr = shift_down(xr, 1).T
        pi = shift_down(xi, 1).T
        for j in range(2):
            rows = slice(j * S5_STATE, (j + 1) * S5_STATE)
            xprev = jnp.concatenate([pr[rows], pi[rows]], axis=0).astype(BF16)
            yt_scr[g0 + j] += _dot(wo_ref[g0 + j], xprev)

    def per_batch(b, carry):
        g0 = S5_BATCH * b
        s = [local_products(g0 + j, j) for j in range(S5_BATCH)]
        for p in range(S5_BATCH // 2):
            scan_pair((S5_BATCH // 2) * b + p, g0 + 2 * p, s[2 * p], s[2 * p + 1])
        return carry

    lax.fori_loop(0, S5_GB // S5_BATCH, per_batch, 0)

    for i in range(t):
        blk = yt_scr[:, i * S5_GROUP:(i + 1) * S5_GROUP, :].reshape(S5_GB * S5_GROUP, nch)
        step_scr[i] = blk.T

    def interleave(blk, carry):
        src = pl.ds(pl.multiple_of(blk * nb, nb), nb)
        per_q = []
        for q in range(nq):
            w = [step_scr[q * SUBLANES + s, src, :].reshape(nsub, SUBLANES, LANES)
                 for s in range(SUBLANES)]
            per_q.append(_sublane_transpose(w))
        x = jnp.stack([jnp.stack([per_q[q][r] for q in range(nq)], axis=1)
                       for r in range(SUBLANES)], axis=1)
        rows = pl.ds(pl.multiple_of(blk * S5_RB, S5_RB), S5_RB)
        o_ref[rows, :] = x.reshape(S5_RB, LANES).astype(o_ref.dtype)
        return carry

    lax.fori_loop(0, (nch * t) // S5_RB, interleave, 0)


def _s5core(proj, tables):
    seq = proj.shape[0]
    w_s, w_o, c_cat, skip, ap_re, ap_im, n_steps = tables
    nch = seq // S5_T
    lanes = S5_GB * S5_GROUP
    grp = lambda i: (i, 0, 0)
    gspec = lambda a: pl.BlockSpec((S5_GB,) + a.shape[1:], grp)
    pspec = lambda a: pl.BlockSpec((S5_GB // 2,) + a.shape[1:], grp)
    return pl.pallas_call(
        functools.partial(_s5core_kernel, n_steps),
        grid=(S5_GROUPS // S5_GB,),
        in_specs=[pl.BlockSpec((seq, lanes), lambda i: (0, i)),
                  gspec(w_s), gspec(w_o), gspec(c_cat), gspec(skip), pspec(ap_re), pspec(ap_im)],
        out_specs=pl.BlockSpec((seq, lanes), lambda i: (0, i)),
        out_shape=jax.ShapeDtypeStruct((seq, S5_WIDTH), BF16),
        scratch_shapes=[
            pltpu.VMEM((S5_T, nch, lanes), F32),
            pltpu.VMEM((S5_GB, S5_TW, nch), BF16),
            pltpu.VMEM((S5_BATCH, S5_TW, S5_TW), BF16),
            pltpu.VMEM((S5_GB, S5_TW, nch), F32),
        ],
        compiler_params=pltpu.CompilerParams(
            dimension_semantics=("parallel",), vmem_limit_bytes=VMEM_LIMIT),
        name="s5_core",
    )(proj, w_s, w_o, c_cat, skip, ap_re, ap_im)


def _gelu_tanh(y):
    return 0.5 * y * (1.0 + jnp.tanh(math.sqrt(2.0 / math.pi) * (y + 0.044715 * (y * y * y))))


def _s5post_block(y_ref, z_ref, wglu_ref, bglu_ref, o_ref):
    g = _gelu_tanh(y_ref[...].astype(F32))
    gate = jax.nn.sigmoid(_dot(g.astype(BF16), wglu_ref[...]) + bglu_ref[...])
    o_ref[...] = (g * gate * jax.nn.silu(z_ref[...].astype(F32))).astype(o_ref.dtype)


def _ret_tables(seq):
    t = RET_T
    inv = ROPE_BASE ** (-jnp.arange(0, RET_QK_HEAD, 2, dtype=F32) / RET_QK_HEAD)
    tile4 = lambda a: jnp.concatenate([a, a, a, a], axis=-1)
    ang_r = jnp.arange(t, dtype=F32)[:, None] * inv[None, :]
    ang_b = (jnp.arange(seq // t, dtype=F32) * float(t))[:, None] * inv[None, :]
    cos_r, sin_r = tile4(jnp.cos(ang_r)), tile4(jnp.sin(ang_r))
    cos_b, sin_b = tile4(jnp.cos(ang_b))[:, None, :], tile4(jnp.sin(ang_b))[:, None, :]
    half = RET_QK_HEAD // 2
    sign = jnp.asarray(np.tile(np.repeat([-1.0, 1.0], half), 2)[None, :], F32)

    log_g = np.log1p(-np.exp2(-5.0 - np.arange(RET_HEADS, dtype=np.float64)))
    i = np.arange(t)
    ci = i // CHUNK
    diff = (i[:, None] - i[None, :]).astype(np.float64)
    same = ci[:, None] == ci[None, :]
    earlier = ci[None, :] < ci[:, None]
    expo = np.where(same, np.abs(diff), diff)
    dmask = np.where((same | earlier)[None], np.exp(expo[None] * log_g[:, None, None]), 0.0)
    xi = np.exp((i + 1.0)[None, :] * log_g[:, None])
    zeta = np.exp((t - 1.0 - i)[None, :] * log_g[:, None])
    xi = np.broadcast_to(xi[:, :, None], (RET_HEADS, t, RET_V_HEAD))
    zeta = np.broadcast_to(zeta[:, :, None], (RET_HEADS, t, RET_V_HEAD))
    block_decay = [float(v) for v in np.exp(t * log_g)]
    return (cos_r, sin_r, cos_b, sin_b, sign, jnp.asarray(dmask, F32), jnp.asarray(xi, F32),
            jnp.asarray(zeta, F32), block_decay)


def _swap_halves(x):
    n = x.shape[1]
    half = RET_QK_HEAD // 2
    fwd = pltpu.roll(x, half, 1)
    bwd = pltpu.roll(x, n - half, 1)
    lane = lax.broadcasted_iota(jnp.int32, x.shape, 1)
    return jnp.where((lane & (RET_QK_HEAD - 1)) < half, bwd, fwd)


def _retention_block(block_decay, q_ref, k_ref, v_ref, z_ref, cosr_ref, sinr_ref, cosb_ref, sinb_ref,
                     sign_ref, dm_ref, xi_ref, zeta_ref, o_ref, state_ref):
    cb = cosb_ref[...]
    sb = sinb_ref[...]
    cos2 = cb * cosr_ref[...] - sb * sinr_ref[...]
    sin2 = (sb * cosr_ref[...] + cb * sinr_ref[...]) * sign_ref[...]
    reps = RET_QK // cos2.shape[1]
    cos = jnp.concatenate([cos2] * reps, axis=1)
    sin = jnp.concatenate([sin2] * reps, axis=1)
    q = q_ref[...].astype(F32)
    k = k_ref[...].astype(F32)
    qr = (q * cos + _swap_halves(q) * sin).astype(BF16)
    kr = ((k * cos + _swap_halves(k) * sin) * (RET_QK_HEAD ** -0.5)).astype(BF16)
    for h in range(RET_HEADS):
        qs = slice(h * RET_QK_HEAD, (h + 1) * RET_QK_HEAD)
        vs = slice(h * RET_V_HEAD, (h + 1) * RET_V_HEAD)
        qh = qr[:, qs]
        kh = kr[:, qs]
        vh = v_ref[:, vs]
        s = lax.dot_general(qh, kh, (((1,), (1,)), ((), ())), preferred_element_type=F32)
        s = s * dm_ref[h]
        inner = _dot(s.astype(BF16), vh)
        st = state_ref[h]
        cross = _dot(qh, st.astype(BF16)) * xi_ref[h]
        vz = (vh.astype(F32) * zeta_ref[h]).astype(BF16)
        kv = lax.dot_general(kh, vz, (((0,), (0,)), ((), ())), preferred_element_type=F32)
        state_ref[h] = st * block_decay[h] + kv
        o = inner + cross
        o = o * lax.rsqrt(jnp.mean(o * o, axis=-1, keepdims=True) + EPS)
        o_ref[:, vs] = (o * jax.nn.silu(z_ref[:, vs].astype(F32))).astype(o_ref.dtype)


def _merge_block(ys_ref, yr_ref, gs0_ref, gs1_ref, gr0_ref, gr1_ref, x_ref, ws_ref, wr_ref, wo_ref,
                 g_ref, o_ref):
    a = _dot(ys_ref[...], ws_ref[...])
    b = _dot(yr_ref[...], wr_ref[...])
    gs = jnp.concatenate([gs0_ref[...], gs1_ref[...]], axis=1).astype(F32)
    gr = jnp.concatenate([gr0_ref[...], gr1_ref[...]], axis=1).astype(F32)
    m = jax.nn.sigmoid(gs) * a + jax.nn.sigmoid(gr) * b
    o = _dot(m.astype(BF16), wo_ref[...])
    o_ref[...] = x_ref[...] + _rms_rows(o, g_ref[...])


def _tail_kernel(block_decay,
                 yssm_ref, zs_ref, q_ref, k_ref, v_ref, zr_ref, cosb_ref, sinb_ref,
                 gs0_ref, gs1_ref, gr0_ref, gr1_ref, x_ref,
                 wglu_ref, bglu_ref, cosr_ref, sinr_ref, sign_ref, dm_ref, xi_ref, zeta_ref,
                 ws_ref, wr_ref, wo_ref, g_ref,
                 o_ref, state_ref, ys_scr, yr_scr):
    @pl.when(pl.program_id(0) == 0)
    def _():
        state_ref[...] = jnp.zeros_like(state_ref)
        ys_scr[...] = jnp.zeros_like(ys_scr)
        yr_scr[...] = jnp.zeros_like(yr_scr)

    _merge_block(ys_scr, yr_scr, gs0_ref, gs1_ref, gr0_ref, gr1_ref, x_ref, ws_ref, wr_ref, wo_ref,
                 g_ref, o_ref)
    _s5post_block(yssm_ref, zs_ref, wglu_ref, bglu_ref, ys_scr)
    _retention_block(block_decay, q_ref, k_ref, v_ref, zr_ref, cosr_ref, sinr_ref, cosb_ref, sinb_ref,
                     sign_ref, dm_ref, xi_ref, zeta_ref, yr_scr, state_ref)


def _tail(y_ssm, proj, x2, w_glu, b_glu, w_s, w_r, w_o, gain):
    seq, d = x2.shape
    t = RET_T
    nblk = seq // t
    cos_r, sin_r, cos_b, sin_b, sign, dmask, xi, zeta, block_decay = _ret_tables(seq)
    blk = 1024
    cur = lambda s: jnp.minimum(s, nblk - 1)
    prv = lambda s: jnp.maximum(s - 1, 0)
    single = pl.Buffered(1)
    const = lambda a: pl.BlockSpec(a.shape, lambda s: (0,) * a.ndim, pipeline_mode=single)
    b_glu2 = b_glu.reshape(1, S5_WIDTH).astype(F32)
    gain2 = gain.reshape(1, d).astype(F32)
    return pl.pallas_call(
        functools.partial(_tail_kernel, block_decay),
        grid=(nblk + 1,),
        in_specs=[
            pl.BlockSpec((t, S5_WIDTH), lambda s: (cur(s), 0)),
            pl.BlockSpec((t, S5_WIDTH), lambda s: (cur(s), _ZS_BLK)),
            pl.BlockSpec((t, RET_QK), lambda s: (cur(s), _Q_BLK512)),
            pl.BlockSpec((t, RET_QK), lambda s: (cur(s), _K_BLK512)),
            pl.BlockSpec((t, RET_WIDTH), lambda s: (cur(s), _V_BLK)),
            pl.BlockSpec((t, RET_WIDTH), lambda s: (cur(s), _ZR_BLK)),
            pl.BlockSpec((None, 1, cos_b.shape[2]), lambda s: (cur(s), 0, 0)),
            pl.BlockSpec((None, 1, sin_b.shape[2]), lambda s: (cur(s), 0, 0)),
            pl.BlockSpec((t, blk), lambda s: (prv(s), _GLS_BLK)),
            pl.BlockSpec((t, blk), lambda s: (prv(s), _GLS_BLK + 1)),
            pl.BlockSpec((t, blk), lambda s: (prv(s), _GLR_BLK)),
            pl.BlockSpec((t, blk), lambda s: (prv(s), _GLR_BLK + 1)),
            pl.BlockSpec((t, d), lambda s: (prv(s), 0)),
            const(w_glu), const(b_glu2), const(cos_r), const(sin_r), const(sign), const(dmask),
            const(xi), const(zeta), const(w_s), const(w_r), const(w_o), const(gain2),
        ],
        out_specs=pl.BlockSpec((t, d), lambda s: (prv(s), 0)),
        out_shape=jax.ShapeDtypeStruct((seq, d), F32),
        scratch_shapes=[
            pltpu.VMEM((RET_HEADS, RET_QK_HEAD, RET_V_HEAD), F32),
            pltpu.VMEM((t, S5_WIDTH), BF16),
            pltpu.VMEM((t, RET_WIDTH), BF16),
        ],
        compiler_params=pltpu.CompilerParams(
            dimension_semantics=("arbitrary",), vmem_limit_bytes=VMEM_LIMIT),
        name="tail",
    )(y_ssm, proj, proj, proj, proj, proj, cos_b, sin_b, proj, proj, proj, proj, x2,
      w_glu, b_glu2, cos_r, sin_r, sign, dmask, xi, zeta, w_s, w_r, w_o, gain2)


def kernel(x, norm_pre, w_in, lam_re, lam_im, log_step, s5_b_re, s5_b_im, s5_c_re, s5_c_im, s5_d,
           w_glu, b_glu, w_proj_s5, w_proj_ret, w_out, norm_post):
    bsz, seq, d = x.shape
    assert bsz == 1 and d == D_MODEL and seq % 1024 == 0
    depth = w_in.shape[0]
    x2 = x.reshape(seq, d)
    for l in range(depth):
        proj = _inproj(x2, norm_pre[l], w_in[l].astype(BF16))
        tables = _s5_tables(lam_re[l], lam_im[l], log_step[l], s5_b_re[l], s5_b_im[l],
                            s5_c_re[l], s5_c_im[l], s5_d[l], seq // S5_T)
        y_ssm = _s5core(proj, tables)
        x2 = _tail(y_ssm, proj, x2, w_glu[l].astype(BF16), b_glu[l], w_proj_s5[l].astype(BF16),
                   w_proj_ret[l].astype(BF16), w_out[l].astype(BF16), norm_post[l])
    return x2.reshape(bsz, seq, d)
```

```python
import functools
import math

import numpy as np
import jax
import jax.numpy as jnp
from jax import lax
from jax.experimental import pallas as pl
from jax.experimental.pallas import tpu as pltpu

F32 = jnp.float32
BF16 = jnp.bfloat16
HIGHEST = lax.Precision.HIGHEST

D_MODEL = 2048
EPS = 1e-6
CHUNK = 64

S5_WIDTH = 1024
S5_GROUP = 16
S5_GROUPS = 64
S5_STATE = 64

RET_HEADS = 8
RET_V_HEAD = 128
RET_QK_HEAD = 64
RET_QK = 512
RET_WIDTH = 1024
ROPE_BASE = 10000.0

_Q_BLK512 = (2 * S5_WIDTH) // 512
_K_BLK512 = (2 * S5_WIDTH + RET_QK) // 512
_ZS_BLK = 1
_V_BLK = (2 * S5_WIDTH + 2 * RET_QK) // 1024
_ZR_BLK = _V_BLK + 1
_GLS_BLK = _ZR_BLK + 1
_GLR_BLK = _GLS_BLK + 2

VMEM_LIMIT = 56 * 1024 * 1024
INPROJ_PIECE = 1280
SUBLANES = 8
LANES = 128

S5_T = 32
S5_GB = 8
S5_TW = S5_T * S5_GROUP
S5_RB = 1024
S5_SCAN_ROWS = 16
S5_BATCH = 4

RET_T = 256


def _dot(a, b):
    return jnp.dot(a, b, preferred_element_type=F32)


def _rms_rows(x, gain):
    ms = jnp.mean(x * x, axis=-1, keepdims=True)
    return x * lax.rsqrt(ms + EPS) * gain


def _inproj_kernel(x_ref, g_ref, w_ref, o_ref, h_ref):
    @pl.when(pl.program_id(1) == 0)
    def _():
        h_ref[...] = _rms_rows(x_ref[...], g_ref[...]).astype(BF16)

    tn = o_ref.shape[1]
    for lo in range(0, tn, INPROJ_PIECE):
        hi = min(lo + INPROJ_PIECE, tn)
        o_ref[:, lo:hi] = _dot(h_ref[...], w_ref[:, lo:hi]).astype(o_ref.dtype)


def _inproj(x2, gain, w_bf16, tm=1024, tn=2304):
    seq, d = x2.shape
    n = w_bf16.shape[1]
    return pl.pallas_call(
        _inproj_kernel,
        grid=(seq // tm, n // tn),
        in_specs=[
            pl.BlockSpec((tm, d), lambda i, j: (i, 0)),
            pl.BlockSpec((1, d), lambda i, j: (0, 0)),
            pl.BlockSpec((d, tn), lambda i, j: (0, j)),
        ],
        out_specs=pl.BlockSpec((tm, tn), lambda i, j: (i, j)),
        out_shape=jax.ShapeDtypeStruct((seq, n), BF16),
        scratch_shapes=[pltpu.VMEM((tm, d), BF16)],
        compiler_params=pltpu.CompilerParams(
            dimension_semantics=("parallel", "arbitrary"), vmem_limit_bytes=VMEM_LIMIT),
        name="inproj",
    )(x2, gain.reshape(1, d).astype(F32), w_bf16)


def _s5_tables(lam_re, lam_im, log_step, b_re, b_im, c_re, c_im, d_skip, n_chunks):
    t = S5_T
    lr = lam_re.astype(F32)
    li = lam_im.astype(F32)
    step = jnp.exp(log_step.astype(F32))[:, None]
    lrs = lr * step
    ang = li * step
    mag = jnp.exp(lrs)
    ab_re = mag * jnp.cos(ang)
    ab_im = mag * jnp.sin(ang)
    den = lr * lr + li * li
    nr = ab_re - 1.0
    f_re = (nr * lr + ab_im * li) / den
    f_im = (ab_im * lr - nr * li) / den
    br = b_re.astype(F32)
    bi = b_im.astype(F32)
    bb_re = f_re[..., None] * br - f_im[..., None] * bi
    bb_im = f_re[..., None] * bi + f_im[..., None] * br
    cr = c_re.astype(F32)
    ci = c_im.astype(F32)

    e_pj = ((t - 1.0) - jnp.arange(t, dtype=F32))[None, None, :]
    m_pj = jnp.exp(e_pj * lrs[:, :, None])
    lane_id = np.arange(S5_TW)
    rep_j = jnp.asarray(lane_id[None, :] // S5_GROUP == np.arange(t)[:, None], F32)
    rep_h = jnp.asarray(lane_id[None, :] % S5_GROUP == np.arange(S5_GROUP)[:, None], F32)
    spread = lambda a, sel: jnp.einsum('gpk,kl->gpl', a, sel, precision=HIGHEST)
    qr = spread(m_pj * jnp.cos(e_pj * ang[:, :, None]), rep_j)
    qi = spread(m_pj * jnp.sin(e_pj * ang[:, :, None]), rep_j)
    br_l = spread(bb_re, rep_h)
    bi_l = spread(bb_im, rep_h)
    w_s = jnp.concatenate([qr * br_l - qi * bi_l, qr * bi_l + qi * br_l], axis=1)

    e_ip = (jnp.arange(t, dtype=F32) + 1.0)[None, :, None]
    m_ip = jnp.exp(e_ip * lrs[:, None, :])
    pr = (m_ip * jnp.cos(e_ip * ang[:, None, :]))[:, :, None, :]
    pi = (m_ip * jnp.sin(e_ip * ang[:, None, :]))[:, :, None, :]
    co_re = (cr[:, None] * pr - ci[:, None] * pi).reshape(S5_GROUPS, S5_TW, S5_STATE)
    co_im = (cr[:, None] * pi + ci[:, None] * pr).reshape(S5_GROUPS, S5_TW, S5_STATE)
    w_o = jnp.concatenate([co_re, -co_im], axis=2).astype(BF16)

    c_cat = jnp.concatenate([cr, -ci], axis=2)
    skip = jnp.broadcast_to(d_skip.astype(F32).reshape(S5_GROUPS, S5_GROUP, 1),
                            (S5_GROUPS, S5_GROUP, LANES))

    n_steps = max(1, int(math.ceil(math.log2(n_chunks))))
    m_t = jnp.exp(float(t) * lrs)
    sq_re, sq_im = [m_t * jnp.cos(float(t) * ang)], [m_t * jnp.sin(float(t) * ang)]
    for _ in range(n_steps - 1):
        r, i = sq_re[-1], sq_im[-1]
        sq_re.append(r * r - i * i)
        sq_im.append(2.0 * r * i)
    def pair_rows(sq):
        a = jnp.stack(sq, axis=0).reshape(n_steps, S5_GROUPS // 2, 2 * S5_STATE)
        a = jnp.transpose(a, (1, 0, 2))
        return jnp.pad(a, ((0, 0), (0, S5_SCAN_ROWS - n_steps), (0, 0)))

    return w_s, w_o, c_cat, skip, pair_rows(sq_re), pair_rows(sq_im), n_steps


def _sublane_transpose(v):
    v = list(v)
    sub = lax.broadcasted_iota(jnp.int32, v[0].shape, 1)
    for k in (4, 2, 1):
        keep = (sub & k) == 0
        for r in range(SUBLANES):
            if r & k:
                continue
            a, b = v[r], v[r | k]
            v[r] = jnp.where(keep, a, pltpu.roll(b, k, 1))
            v[r | k] = jnp.where(keep, pltpu.roll(a, SUBLANES - k, 1), b)
    return v


def _s5core_kernel(n_steps, u_ref, ws_ref, wo_ref, ccat_ref, skip_ref, apr_ref, api_ref,
                   o_ref, step_scr, rhs_scr, mt_scr, yt_scr):
    t = S5_T
    nch = step_scr.shape[1]
    nq = t // SUBLANES
    nb = S5_RB // t
    nsub = nb // SUBLANES

    def deinterleave(blk, carry):
        rows = pl.ds(pl.multiple_of(blk * S5_RB, S5_RB), S5_RB)
        x = u_ref[rows, :].astype(F32).reshape(nsub, SUBLANES, nq, SUBLANES, LANES)
        dst = pl.ds(pl.multiple_of(blk * nb, nb), nb)
        for q in range(nq):
            w = _sublane_transpose([x[:, r, q] for r in range(SUBLANES)])
            for s in range(SUBLANES):
                step_scr[q * SUBLANES + s, dst, :] = w[s].reshape(nb, LANES)
        return carry

    lax.fori_loop(0, (nch * t) // S5_RB, deinterleave, 0)

    for i in range(t):
        rhs_scr[:, i * S5_GROUP:(i + 1) * S5_GROUP, :] = (
            step_scr[i].T.astype(BF16).reshape(S5_GB, S5_GROUP, nch))

    tail_lane = lax.broadcasted_iota(jnp.int32, (S5_GROUP, LANES), 1)
    tail_row = lax.broadcasted_iota(jnp.int32, (S5_GROUP, LANES), 0)
    lag0 = tail_lane == tail_row + (LANES - S5_GROUP)
    zero_half = jnp.zeros((S5_GROUP, S5_TW), F32)
    per_tile = LANES // S5_GROUP
    chunk_row = lax.broadcasted_iota(jnp.int32, (nch, LANES), 0)

    def shift_down(v, k):
        if k % SUBLANES == 0:
            return jnp.concatenate([jnp.zeros((k, LANES), F32), v[:nch - k]], axis=0)
        return jnp.where(chunk_row >= k, pltpu.roll(v, k, 0), 0.0)

    def local_products(g, slot):
        mt = mt_scr.at[slot]
        ws = ws_ref[g]
        strip = jnp.dot(ccat_ref[g], ws, precision=HIGHEST, preferred_element_type=F32)
        tail = strip[:, S5_TW - LANES:] + jnp.where(lag0, skip_ref[g], 0.0)
        full = jnp.concatenate([strip[:, :S5_TW - LANES], tail, zero_half], axis=1)
        rot = [full] + [pltpu.roll(full, 2 * S5_TW - S5_GROUP * r, 1) for r in range(1, per_tile)]
        for i in range(t):
            a, r = divmod(t - 1 - i, per_tile)
            mt[i * S5_GROUP:(i + 1) * S5_GROUP, :] = (
                rot[r][:, a * LANES:a * LANES + S5_TW].astype(BF16))
        rhs = rhs_scr[g]
        yt_scr[g] = _dot(mt[...], rhs)
        return _dot(ws.astype(BF16), rhs)

    def scan_pair(gp, g0, s0, s1):
        xr = jnp.concatenate([s0[:S5_STATE], s1[:S5_STATE]], axis=0).T
        xi = jnp.concatenate([s0[S5_STATE:], s1[S5_STATE:]], axis=0).T
        apr = apr_ref[gp]
        api = api_ref[gp]
        for k in range(n_steps):
            ar = apr[k:k + 1, :]
            ai = api[k:k + 1, :]
            sr = shift_down(xr, 1 << k)
            si = shift_down(xi, 1 << k)
            xr, xi = xr + (ar * sr - ai * si), xi + (ar * si + ai * sr)
        pr = shift_down(xr, 1).T
        pi = shift_down(xi, 1).T
        for j in range(2):
            rows = slice(j * S5_STATE, (j + 1) * S5_STATE)
            xprev = jnp.concatenate([pr[rows], pi[rows]], axis=0).astype(BF16)
            yt_scr[g0 + j] += _dot(wo_ref[g0 + j], xprev)

    def per_batch(b, carry):
        g0 = S5_BATCH * b
        s = [local_products(g0 + j, j) for j in range(S5_BATCH)]
        for p in range(S5_BATCH // 2):
            scan_pair((S5_BATCH // 2) * b + p, g0 + 2 * p, s[2 * p], s[2 * p + 1])
        return carry

    lax.fori_loop(0, S5_GB // S5_BATCH, per_batch, 0)

    for i in range(t):
        blk = yt_scr[:, i * S5_GROUP:(i + 1) * S5_GROUP, :].reshape(S5_GB * S5_GROUP, nch)
        step_scr[i] = blk.T

    def interleave(blk, carry):
        src = pl.ds(pl.multiple_of(blk * nb, nb), nb)
        per_q = []
        for q in range(nq):
            w = [step_scr[q * SUBLANES + s, src, :].reshape(nsub, SUBLANES, LANES)
                 for s in range(SUBLANES)]
            per_q.append(_sublane_transpose(w))
        x = jnp.stack([jnp.stack([per_q[q][r] for q in range(nq)], axis=1)
                       for r in range(SUBLANES)], axis=1)
        rows = pl.ds(pl.multiple_of(blk * S5_RB, S5_RB), S5_RB)
        o_ref[rows, :] = x.reshape(S5_RB, LANES).astype(o_ref.dtype)
        return carry

    lax.fori_loop(0, (nch * t) // S5_RB, interleave, 0)


def _s5core(proj, tables):
    seq = proj.shape[0]
    w_s, w_o, c_cat, skip, ap_re, ap_im, n_steps = tables
    nch = seq // S5_T
    lanes = S5_GB * S5_GROUP
    grp = lambda i: (i, 0, 0)
    gspec = lambda a: pl.BlockSpec((S5_GB,) + a.shape[1:], grp)
    pspec = lambda a: pl.BlockSpec((S5_GB // 2,) + a.shape[1:], grp)
    return pl.pallas_call(
        functools.partial(_s5core_kernel, n_steps),
        grid=(S5_GROUPS // S5_GB,),
        in_specs=[pl.BlockSpec((seq, lanes), lambda i: (0, i)),
                  gspec(w_s), gspec(w_o), gspec(c_cat), gspec(skip), pspec(ap_re), pspec(ap_im)],
        out_specs=pl.BlockSpec((seq, lanes), lambda i: (0, i)),
        out_shape=jax.ShapeDtypeStruct((seq, S5_WIDTH), BF16),
        scratch_shapes=[
            pltpu.VMEM((S5_T, nch, lanes), F32),
            pltpu.VMEM((S5_GB, S5_TW, nch), BF16),
            pltpu.VMEM((S5_BATCH, S5_TW, S5_TW), BF16),
            pltpu.VMEM((S5_GB, S5_TW, nch), F32),
        ],
        compiler_params=pltpu.CompilerParams(
            dimension_semantics=("parallel",), vmem_limit_bytes=VMEM_LIMIT),
        name="s5_core",
    )(proj, w_s, w_o, c_cat, skip, ap_re, ap_im)


def _gelu_tanh(y):
    return 0.5 * y * (1.0 + jnp.tanh(math.sqrt(2.0 / math.pi) * (y + 0.044715 * (y * y * y))))


def _s5post_block(y_ref, z_ref, wglu_ref, bglu_ref, o_ref):
    g = _gelu_tanh(y_ref[...].astype(F32))
    gate = jax.nn.sigmoid(_dot(g.astype(BF16), wglu_ref[...]) + bglu_ref[...])
    o_ref[...] = (g * gate * jax.nn.silu(z_ref[...].astype(F32))).astype(o_ref.dtype)


def _ret_tables(seq):
    t = RET_T
    inv = ROPE_BASE ** (-jnp.arange(0, RET_QK_HEAD, 2, dtype=F32) / RET_QK_HEAD)
    tile4 = lambda a: jnp.concatenate([a, a, a, a], axis=-1)
    ang_r = jnp.arange(t, dtype=F32)[:, None] * inv[None, :]
    ang_b = (jnp.arange(seq // t, dtype=F32) * float(t))[:, None] * inv[None, :]
    cos_r, sin_r = tile4(jnp.cos(ang_r)), tile4(jnp.sin(ang_r))
    cos_b, sin_b = tile4(jnp.cos(ang_b))[:, None, :], tile4(jnp.sin(ang_b))[:, None, :]
    half = RET_QK_HEAD // 2
    sign = jnp.asarray(np.tile(np.repeat([-1.0, 1.0], half), 2)[None, :], F32)

    log_g = np.log1p(-np.exp2(-5.0 - np.arange(RET_HEADS, dtype=np.float64)))
    i = np.arange(t)
    ci = i // CHUNK
    diff = (i[:, None] - i[None, :]).astype(np.float64)
    same = ci[:, None] == ci[None, :]
    earlier = ci[None, :] < ci[:, None]
    expo = np.where(same, np.abs(diff), diff)
    dmask = np.where((same | earlier)[None], np.exp(expo[None] * log_g[:, None, None]), 0.0)
    xi = np.exp((i + 1.0)[None, :] * log_g[:, None])
    zeta = np.exp((t - 1.0 - i)[None, :] * log_g[:, None])
    xi = np.broadcast_to(xi[:, :, None], (RET_HEADS, t, RET_V_HEAD))
    zeta = np.broadcast_to(zeta[:, :, None], (RET_HEADS, t, RET_V_HEAD))
    block_decay = [float(v) for v in np.exp(t * log_g)]
    return (cos_r, sin_r, cos_b, sin_b, sign, jnp.asarray(dmask, F32), jnp.asarray(xi, F32),
            jnp.asarray(zeta, F32), block_decay)


def _swap_halves(x):
    n = x.shape[1]
    half = RET_QK_HEAD // 2
    fwd = pltpu.roll(x, half, 1)
    bwd = pltpu.roll(x, n - half, 1)
    lane = lax.broadcasted_iota(jnp.int32, x.shape, 1)
    return jnp.where((lane & (RET_QK_HEAD - 1)) < half, bwd, fwd)


def _retention_block(block_decay, q_ref, k_ref, v_ref, z_ref, cosr_ref, sinr_ref, cosb_ref, sinb_ref,
                     sign_ref, dm_ref, xi_ref, zeta_ref, o_ref, state_ref):
    cb = cosb_ref[...]
    sb = sinb_ref[...]
    cos2 = cb * cosr_ref[...] - sb * sinr_ref[...]
    sin2 = (sb * cosr_ref[...] + cb * sinr_ref[...]) * sign_ref[...]
    reps = RET_QK // cos2.shape[1]
    cos = jnp.concatenate([cos2] * reps, axis=1)
    sin = jnp.concatenate([sin2] * reps, axis=1)
    q = q_ref[...].astype(F32)
    k = k_ref[...].astype(F32)
    qr = (q * cos + _swap_halves(q) * sin).astype(BF16)
    kr = ((k * cos + _swap_halves(k) * sin) * (RET_QK_HEAD ** -0.5)).astype(BF16)
    for h in range(RET_HEADS):
        qs = slice(h * RET_QK_HEAD, (h + 1) * RET_QK_HEAD)
        vs = slice(h * RET_V_HEAD, (h + 1) * RET_V_HEAD)
        qh = qr[:, qs]
        kh = kr[:, qs]
        vh = v_ref[:, vs]
        s = lax.dot_general(qh, kh, (((1,), (1,)), ((), ())), preferred_element_type=F32)
        s = s * dm_ref[h]
        inner = _dot(s.astype(BF16), vh)
        st = state_ref[h]
        cross = _dot(qh, st.astype(BF16)) * xi_ref[h]
        vz = (vh.astype(F32) * zeta_ref[h]).astype(BF16)
        kv = lax.dot_general(kh, vz, (((0,), (0,)), ((), ())), preferred_element_type=F32)
        state_ref[h] = st * block_decay[h] + kv
        o = inner + cross
        o = o * lax.rsqrt(jnp.mean(o * o, axis=-1, keepdims=True) + EPS)
        o_ref[:, vs] = (o * jax.nn.silu(z_ref[:, vs].astype(F32))).astype(o_ref.dtype)


def _merge_block(ys_ref, yr_ref, gs0_ref, gs1_ref, gr0_ref, gr1_ref, x_ref, ws_ref, wr_ref, wo_ref,
                 g_ref, o_ref):
    a = _dot(ys_ref[...], ws_ref[...])
    b = _dot(yr_ref[...], wr_ref[...])
    gs = jnp.concatenate([gs0_ref[...], gs1_ref[...]], axis=1).astype(F32)
    gr = jnp.concatenate([gr0_ref[...], gr1_ref[...]], axis=1).astype(F32)
    m = jax.nn.sigmoid(gs) * a + jax.nn.sigmoid(gr) * b
    o = _dot(m.astype(BF16), wo_ref[...])
    o_ref[...] = x_ref[...] + _rms_rows(o, g_ref[...])


def _tail_kernel(block_decay,
                 yssm_ref, zs_ref, q_ref, k_ref, v_ref, zr_ref, cosb_ref, sinb_ref,
                 gs0_ref, gs1_ref, gr0_ref, gr1_ref, x_ref,
                 wglu_ref, bglu_ref, cosr_ref, sinr_ref, sign_ref, dm_ref, xi_ref, zeta_ref,
                 ws_ref, wr_ref, wo_ref, g_ref,
                 o_ref, state_ref, ys_scr, yr_scr):
    @pl.when(pl.program_id(0) == 0)
    def _():
        state_ref[...] = jnp.zeros_like(state_ref)
        ys_scr[...] = jnp.zeros_like(ys_scr)
        yr_scr[...] = jnp.zeros_like(yr_scr)

    _merge_block(ys_scr, yr_scr, gs0_ref, gs1_ref, gr0_ref, gr1_ref, x_ref, ws_ref, wr_ref, wo_ref,
                 g_ref, o_ref)
    _s5post_block(yssm_ref, zs_ref, wglu_ref, bglu_ref, ys_scr)
    _retention_block(block_decay, q_ref, k_ref, v_ref, zr_ref, cosr_ref, sinr_ref, cosb_ref, sinb_ref,
                     sign_ref, dm_ref, xi_ref, zeta_ref, yr_scr, state_ref)


def _tail(y_ssm, proj, x2, w_glu, b_glu, w_s, w_r, w_o, gain):
    seq, d = x2.shape
    t = RET_T
    nblk = seq // t
    cos_r, sin_r, cos_b, sin_b, sign, dmask, xi, zeta, block_decay = _ret_tables(seq)
    blk = 1024
    cur = lambda s: jnp.minimum(s, nblk - 1)
    prv = lambda s: jnp.maximum(s - 1, 0)
    single = pl.Buffered(1)
    const = lambda a: pl.BlockSpec(a.shape, lambda s: (0,) * a.ndim, pipeline_mode=single)
    b_glu2 = b_glu.reshape(1, S5_WIDTH).astype(F32)
    gain2 = gain.reshape(1, d).astype(F32)
    return pl.pallas_call(
        functools.partial(_tail_kernel, block_decay),
        grid=(nblk + 1,),
        in_specs=[
            pl.BlockSpec((t, S5_WIDTH), lambda s: (cur(s), 0)),
            pl.BlockSpec((t, S5_WIDTH), lambda s: (cur(s), _ZS_BLK)),
            pl.BlockSpec((t, RET_QK), lambda s: (cur(s), _Q_BLK512)),
            pl.BlockSpec((t, RET_QK), lambda s: (cur(s), _K_BLK512)),
            pl.BlockSpec((t, RET_WIDTH), lambda s: (cur(s), _V_BLK)),
            pl.BlockSpec((t, RET_WIDTH), lambda s: (cur(s), _ZR_BLK)),
            pl.BlockSpec((None, 1, cos_b.shape[2]), lambda s: (cur(s), 0, 0)),
            pl.BlockSpec((None, 1, sin_b.shape[2]), lambda s: (cur(s), 0, 0)),
            pl.BlockSpec((t, blk), lambda s: (prv(s), _GLS_BLK)),
            pl.BlockSpec((t, blk), lambda s: (prv(s), _GLS_BLK + 1)),
            pl.BlockSpec((t, blk), lambda s: (prv(s), _GLR_BLK)),
            pl.BlockSpec((t, blk), lambda s: (prv(s), _GLR_BLK + 1)),
            pl.BlockSpec((t, d), lambda s: (prv(s), 0)),
            const(w_glu), const(b_glu2), const(cos_r), const(sin_r), const(sign), const(dmask),
            const(xi), const(zeta), const(w_s), const(w_r), const(w_o), const(gain2),
        ],
        out_specs=pl.BlockSpec((t, d), lambda s: (prv(s), 0)),
        out_shape=jax.ShapeDtypeStruct((seq, d), F32),
        scratch_shapes=[
            pltpu.VMEM((RET_HEADS, RET_QK_HEAD, RET_V_HEAD), F32),
            pltpu.VMEM((t, S5_WIDTH), BF16),
            pltpu.VMEM((t, RET_WIDTH), BF16),
        ],
        compiler_params=pltpu.CompilerParams(
            dimension_semantics=("arbitrary",), vmem_limit_bytes=VMEM_LIMIT),
        name="tail",
    )(y_ssm, proj, proj, proj, proj, proj, cos_b, sin_b, proj, proj, proj, proj, x2,
      w_glu, b_glu2, cos_r, sin_r, sign, dmask, xi, zeta, w_s, w_r, w_o, gain2)


def kernel(x, norm_pre, w_in, lam_re, lam_im, log_step, s5_b_re, s5_b_im, s5_c_re, s5_c_im, s5_d,
           w_glu, b_glu, w_proj_s5, w_proj_ret, w_out, norm_post):
    bsz, seq, d = x.shape
    assert bsz == 1 and d == D_MODEL and seq % 1024 == 0
    depth = w_in.shape[0]
    x2 = x.reshape(seq, d)
    for l in range(depth):
        proj = _inproj(x2, norm_pre[l], w_in[l].astype(BF16))
        tables = _s5_tables(lam_re[l], lam_im[l], log_step[l], s5_b_re[l], s5_b_im[l],
                            s5_c_re[l], s5_c_im[l], s5_d[l], seq // S5_T)
        y_ssm = _s5core(proj, tables)
        x2 = _tail(y_ssm, proj, x2, w_glu[l].astype(BF16), b_glu[l], w_proj_s5[l].astype(BF16),
                   w_proj_ret[l].astype(BF16), w_out[l].astype(BF16), norm_post[l])
    return x2.reshape(bsz, seq, d)
```

```python
import functools
import math

import numpy as np
import jax
import jax.numpy as jnp
from jax import lax
from jax.experimental import pallas as pl
from jax.experimental.pallas import tpu as pltpu

F32 = jnp.float32
BF16 = jnp.bfloat16
HIGHEST = lax.Precision.HIGHEST

D_MODEL = 2048
EPS = 1e-6
CHUNK = 64

S5_WIDTH = 1024
S5_GROUP = 16
S5_GROUPS = 64
S5_STATE = 64

RET_HEADS = 8
RET_V_HEAD = 128
RET_QK_HEAD = 64
RET_QK = 512
RET_WIDTH = 1024
ROPE_BASE = 10000.0

_Q_BLK512 = (2 * S5_WIDTH) // 512
_K_BLK512 = (2 * S5_WIDTH + RET_QK) // 512
_ZS_BLK = 1
_V_BLK = (2 * S5_WIDTH + 2 * RET_QK) // 1024
_ZR_BLK = _V_BLK + 1
_GLS_BLK = _ZR_BLK + 1
_GLR_BLK = _GLS_BLK + 2

VMEM_LIMIT = 56 * 1024 * 1024
INPROJ_PIECE = 1280
SUBLANES = 8
LANES = 128

S5_T = 32
S5_GB = 8
S5_TW = S5_T * S5_GROUP
S5_RB = 1024
S5_SCAN_ROWS = 16
S5_BATCH = 4

RET_T = 256


def _dot(a, b):
    return jnp.dot(a, b, preferred_element_type=F32)


def _rms_rows(x, gain):
    ms = jnp.mean(x * x, axis=-1, keepdims=True)
    return x * lax.rsqrt(ms + EPS) * gain


def _inproj_kernel(x_ref, g_ref, w_ref, o_ref, h_ref):
    @pl.when(pl.program_id(1) == 0)
    def _():
        h_ref[...] = _rms_rows(x_ref[...], g_ref[...]).astype(BF16)

    tn = o_ref.shape[1]
    for lo in range(0, tn, INPROJ_PIECE):
        hi = min(lo + INPROJ_PIECE, tn)
        o_ref[:, lo:hi] = _dot(h_ref[...], w_ref[:, lo:hi]).astype(o_ref.dtype)


def _inproj(x2, gain, w_bf16, tm=1024, tn=2304):
    seq, d = x2.shape
    n = w_bf16.shape[1]
    return pl.pallas_call(
        _inproj_kernel,
        grid=(seq // tm, n // tn),
        in_specs=[
            pl.BlockSpec((tm, d), lambda i, j: (i, 0)),
            pl.BlockSpec((1, d), lambda i, j: (0, 0)),
            pl.BlockSpec((d, tn), lambda i, j: (0, j)),
        ],
        out_specs=pl.BlockSpec((tm, tn), lambda i, j: (i, j)),
        out_shape=jax.ShapeDtypeStruct((seq, n), BF16),
        scratch_shapes=[pltpu.VMEM((tm, d), BF16)],
        compiler_params=pltpu.CompilerParams(
            dimension_semantics=("parallel", "arbitrary"), vmem_limit_bytes=VMEM_LIMIT),
        name="inproj",
    )(x2, gain.reshape(1, d).astype(F32), w_bf16)


def _s5_tables(lam_re, lam_im, log_step, b_re, b_im, c_re, c_im, d_skip, n_chunks):
    t = S5_T
    lr = lam_re.astype(F32)
    li = lam_im.astype(F32)
    step = jnp.exp(log_step.astype(F32))[:, None]
    lrs = lr * step
    ang = li * step
    mag = jnp.exp(lrs)
    ab_re = mag * jnp.cos(ang)
    ab_im = mag * jnp.sin(ang)
    den = lr * lr + li * li
    nr = ab_re - 1.0
    f_re = (nr * lr + ab_im * li) / den
    f_im = (ab_im * lr - nr * li) / den
    br = b_re.astype(F32)
    bi = b_im.astype(F32)
    bb_re = f_re[..., None] * br - f_im[..., None] * bi
    bb_im = f_re[..., None] * bi + f_im[..., None] * br
    cr = c_re.astype(F32)
    ci = c_im.astype(F32)

    n_bits = int(math.log2(t))
    bit_re, bit_im = [ab_re], [ab_im]
    for _ in range(n_bits - 1):
        r, i = bit_re[-1], bit_im[-1]
        bit_re.append(r * r - i * i)
        bit_im.append(2.0 * r * i)
    pad_bits = ((0, 0), (0, 0), (0, SUBLANES - n_bits))
    abit_re = jnp.pad(jnp.stack(bit_re, axis=-1), pad_bits)
    abit_im = jnp.pad(jnp.stack(bit_im, axis=-1), pad_bits)
    bbt_re = jnp.tile(bb_re, (1, 1, LANES // S5_GROUP))
    bbt_im = jnp.tile(bb_im, (1, 1, LANES // S5_GROUP))
    e_ip = (jnp.arange(t, dtype=F32) + 1.0)[None, :, None]
    m_ip = jnp.exp(e_ip * lrs[:, None, :])
    pr = m_ip * jnp.cos(e_ip * ang[:, None, :])
    pi = m_ip * jnp.sin(e_ip * ang[:, None, :])
    p1 = jnp.concatenate([pr, -pi], axis=2)
    p2 = jnp.concatenate([-pi, -pr], axis=2)
    c_rr = jnp.concatenate([cr, cr], axis=2)
    c_ii = jnp.concatenate([ci, ci], axis=2)

    c_cat = jnp.concatenate([cr, -ci], axis=2)
    skip = jnp.broadcast_to(d_skip.astype(F32).reshape(S5_GROUPS, S5_GROUP, 1),
                            (S5_GROUPS, S5_GROUP, LANES))

    n_steps = max(1, int(math.ceil(math.log2(n_chunks))))
    m_t = jnp.exp(float(t) * lrs)
    sq_re, sq_im = [m_t * jnp.cos(float(t) * ang)], [m_t * jnp.sin(float(t) * ang)]
    for _ in range(n_steps - 1):
        r, i = sq_re[-1], sq_im[-1]
        sq_re.append(r * r - i * i)
        sq_im.append(2.0 * r * i)
    def pair_rows(sq):
        a = jnp.stack(sq, axis=0).reshape(n_steps, S5_GROUPS // 2, 2 * S5_STATE)
        a = jnp.transpose(a, (1, 0, 2))
        return jnp.pad(a, ((0, 0), (0, S5_SCAN_ROWS - n_steps), (0, 0)))

    per_group = (abit_re, abit_im, bbt_re, bbt_im, p1, p2, c_rr, c_ii, c_cat, skip)
    return per_group, pair_rows(sq_re), pair_rows(sq_im), n_steps


def _sublane_transpose(v):
    v = list(v)
    sub = lax.broadcasted_iota(jnp.int32, v[0].shape, 1)
    for k in (4, 2, 1):
        keep = (sub & k) == 0
        for r in range(SUBLANES):
            if r & k:
                continue
            a, b = v[r], v[r | k]
            v[r] = jnp.where(keep, a, pltpu.roll(b, k, 1))
            v[r | k] = jnp.where(keep, pltpu.roll(a, SUBLANES - k, 1), b)
    return v


def _s5core_kernel(n_steps, u_ref, abr_ref, abi_ref, bbr_ref, bbi_ref, p1_ref, p2_ref, crr_ref, cii_ref,
                   ccat_ref, skip_ref, apr_ref, api_ref, o_ref, step_scr, rhs_scr, mt_scr, yt_scr):
    t = S5_T
    nch = step_scr.shape[1]
    nq = t // SUBLANES
    nb = S5_RB // t
    nsub = nb // SUBLANES

    def deinterleave(blk, carry):
        rows = pl.ds(pl.multiple_of(blk * S5_RB, S5_RB), S5_RB)
        x = u_ref[rows, :].astype(F32).reshape(nsub, SUBLANES, nq, SUBLANES, LANES)
        dst = pl.ds(pl.multiple_of(blk * nb, nb), nb)
        for q in range(nq):
            w = _sublane_transpose([x[:, r, q] for r in range(SUBLANES)])
            for s in range(SUBLANES):
                step_scr[q * SUBLANES + s, dst, :] = w[s].reshape(nb, LANES)
        return carry

    lax.fori_loop(0, (nch * t) // S5_RB, deinterleave, 0)

    for i in range(t):
        rhs_scr[:, i * S5_GROUP:(i + 1) * S5_GROUP, :] = (
            step_scr[i].T.astype(BF16).reshape(S5_GB, S5_GROUP, nch))

    tail_lane = lax.broadcasted_iota(jnp.int32, (S5_GROUP, LANES), 1)
    tail_row = lax.broadcasted_iota(jnp.int32, (S5_GROUP, LANES), 0)
    lag0 = tail_lane == tail_row + (LANES - S5_GROUP)
    zero_half = jnp.zeros((S5_GROUP, S5_TW), F32)
    per_tile = LANES // S5_GROUP
    chunk_row = lax.broadcasted_iota(jnp.int32, (nch, LANES), 0)

    def shift_down(v, k):
        if k % SUBLANES == 0:
            return jnp.concatenate([jnp.zeros((k, LANES), F32), v[:nch - k]], axis=0)
        return jnp.where(chunk_row >= k, pltpu.roll(v, k, 0), 0.0)

    lane_expo = (t - 1) - lax.shift_right_logical(
        lax.broadcasted_iota(jnp.int32, (S5_STATE, S5_TW), 1), int(math.log2(S5_GROUP)))
    lane_reps = S5_TW // LANES

    def state_weights(g):
        abr = abr_ref[g]
        abi = abi_ref[g]
        pw_re = pw_im = None
        for b in range(int(math.log2(t))):
            bit = (lax.shift_right_logical(lane_expo, b) & 1) == 1
            f_re = jnp.where(bit, abr[:, b:b + 1], 1.0)
            f_im = jnp.where(bit, abi[:, b:b + 1], 0.0)
            if pw_re is None:
                pw_re, pw_im = f_re, f_im
            else:
                pw_re, pw_im = pw_re * f_re - pw_im * f_im, pw_re * f_im + pw_im * f_re
        br = jnp.concatenate([bbr_ref[g]] * lane_reps, axis=1)
        bi = jnp.concatenate([bbi_ref[g]] * lane_reps, axis=1)
        return jnp.concatenate([pw_re * br - pw_im * bi, pw_re * bi + pw_im * br], axis=0)

    def output_weights(g):
        crr = crr_ref[g]
        cii = cii_ref[g]
        p1 = p1_ref[g]
        p2 = p2_ref[g]
        return jnp.concatenate(
            [(crr * p1[i:i + 1, :] + cii * p2[i:i + 1, :]).astype(BF16) for i in range(t)], axis=0)

    def local_products(g, slot):
        mt = mt_scr.at[slot]
        ws = state_weights(g)
        strip = jnp.dot(ccat_ref[g], ws, precision=HIGHEST, preferred_element_type=F32)
        tail = strip[:, S5_TW - LANES:] + jnp.where(lag0, skip_ref[g], 0.0)
        full = jnp.concatenate([strip[:, :S5_TW - LANES], tail, zero_half], axis=1)
        rot = [full] + [pltpu.roll(full, 2 * S5_TW - S5_GROUP * r, 1) for r in range(1, per_tile)]
        for i in range(t):
            a, r = divmod(t - 1 - i, per_tile)
            mt[i * S5_GROUP:(i + 1) * S5_GROUP, :] = (
                rot[r][:, a * LANES:a * LANES + S5_TW].astype(BF16))
        rhs = rhs_scr[g]
        yt_scr[g] = _dot(mt[...], rhs)
        return _dot(ws.astype(BF16), rhs)

    def scan_pair(gp, g0, s0, s1):
        xr = jnp.concatenate([s0[:S5_STATE], s1[:S5_STATE]], axis=0).T
        xi = jnp.concatenate([s0[S5_STATE:], s1[S5_STATE:]], axis=0).T
        apr = apr_ref[gp]
        api = api_ref[gp]
        for k in range(n_steps):
            ar = apr[k:k + 1, :]
            ai = api[k:k + 1, :]
            sr = shift_down(xr, 1 << k)
            si = shift_down(xi, 1 << k)
            xr, xi = xr + (ar * sr - ai * si), xi + (ar * si + ai * sr)
        pr = shift_down(xr, 1).T
        pi = shift_down(xi, 1).T
        for j in range(2):
            rows = slice(j * S5_STATE, (j + 1) * S5_STATE)
            xprev = jnp.concatenate([pr[rows], pi[rows]], axis=0).astype(BF16)
            yt_scr[g0 + j] += _dot(output_weights(g0 + j), xprev)

    def per_batch(b, carry):
        g0 = S5_BATCH * b
        s = [local_products(g0 + j, j) for j in range(S5_BATCH)]
        for p in range(S5_BATCH // 2):
            scan_pair((S5_BATCH // 2) * b + p, g0 + 2 * p, s[2 * p], s[2 * p + 1])
        return carry

    lax.fori_loop(0, S5_GB // S5_BATCH, per_batch, 0)

    for i in range(t):
        blk = yt_scr[:, i * S5_GROUP:(i + 1) * S5_GROUP, :].reshape(S5_GB * S5_GROUP, nch)
        step_scr[i] = blk.T

    def interleave(blk, carry):
        src = pl.ds(pl.multiple_of(blk * nb, nb), nb)
        per_q = []
        for q in range(nq):
            w = [step_scr[q * SUBLANES + s, src, :].reshape(nsub, SUBLANES, LANES)
                 for s in range(SUBLANES)]
            per_q.append(_sublane_transpose(w))
        x = jnp.stack([jnp.stack([per_q[q][r] for q in range(nq)], axis=1)
                       for r in range(SUBLANES)], axis=1)
        rows = pl.ds(pl.multiple_of(blk * S5_RB, S5_RB), S5_RB)
        o_ref[rows, :] = x.reshape(S5_RB, LANES).astype(o_ref.dtype)
        return carry

    lax.fori_loop(0, (nch * t) // S5_RB, interleave, 0)


def _s5core(proj, tables):
    seq = proj.shape[0]
    per_group, ap_re, ap_im, n_steps = tables
    nch = seq // S5_T
    lanes = S5_GB * S5_GROUP
    grp = lambda i: (i, 0, 0)
    gspec = lambda a: pl.BlockSpec((S5_GB,) + a.shape[1:], grp)
    pspec = lambda a: pl.BlockSpec((S5_GB // 2,) + a.shape[1:], grp)
    return pl.pallas_call(
        functools.partial(_s5core_kernel, n_steps),
        grid=(S5_GROUPS // S5_GB,),
        in_specs=([pl.BlockSpec((seq, lanes), lambda i: (0, i))] + [gspec(a) for a in per_group]
                  + [pspec(ap_re), pspec(ap_im)]),
        out_specs=pl.BlockSpec((seq, lanes), lambda i: (0, i)),
        out_shape=jax.ShapeDtypeStruct((seq, S5_WIDTH), BF16),
        scratch_shapes=[
            pltpu.VMEM((S5_T, nch, lanes), F32),
            pltpu.VMEM((S5_GB, S5_TW, nch), BF16),
            pltpu.VMEM((S5_BATCH, S5_TW, S5_TW), BF16),
            pltpu.VMEM((S5_GB, S5_TW, nch), F32),
        ],
        compiler_params=pltpu.CompilerParams(
            dimension_semantics=("parallel",), vmem_limit_bytes=VMEM_LIMIT),
        name="s5_core",
    )(proj, *per_group, ap_re, ap_im)


def _gelu_tanh(y):
    return 0.5 * y * (1.0 + jnp.tanh(math.sqrt(2.0 / math.pi) * (y + 0.044715 * (y * y * y))))


def _s5post_block(y_ref, z_ref, wglu_ref, bglu_ref, o_ref):
    g = _gelu_tanh(y_ref[...].astype(F32))
    gate = jax.nn.sigmoid(_dot(g.astype(BF16), wglu_ref[...]) + bglu_ref[...])
    o_ref[...] = (g * gate * jax.nn.silu(z_ref[...].astype(F32))).astype(o_ref.dtype)


def _ret_tables(seq):
    t = RET_T
    inv = ROPE_BASE ** (-jnp.arange(0, RET_QK_HEAD, 2, dtype=F32) / RET_QK_HEAD)
    tile4 = lambda a: jnp.concatenate([a, a, a, a], axis=-1)
    ang_r = jnp.arange(t, dtype=F32)[:, None] * inv[None, :]
    ang_b = (jnp.arange(seq // t, dtype=F32) * float(t))[:, None] * inv[None, :]
    cos_r, sin_r = tile4(jnp.cos(ang_r)), tile4(jnp.sin(ang_r))
    cos_b, sin_b = tile4(jnp.cos(ang_b))[:, None, :], tile4(jnp.sin(ang_b))[:, None, :]
    half = RET_QK_HEAD // 2
    sign = jnp.asarray(np.tile(np.repeat([-1.0, 1.0], half), 2)[None, :], F32)

    log_g = np.log1p(-np.exp2(-5.0 - np.arange(RET_HEADS, dtype=np.float64)))
    i = np.arange(t)
    ci = i // CHUNK
    diff = (i[:, None] - i[None, :]).astype(np.float64)
    same = ci[:, None] == ci[None, :]
    earlier = ci[None, :] < ci[:, None]
    expo = np.where(same, np.abs(diff), diff)
    dmask = np.where((same | earlier)[None], np.exp(expo[None] * log_g[:, None, None]), 0.0)
    xi = np.exp((i + 1.0)[None, :] * log_g[:, None])
    zeta = np.exp((t - 1.0 - i)[None, :] * log_g[:, None])
    xi = np.broadcast_to(xi[:, :, None], (RET_HEADS, t, RET_V_HEAD))
    zeta = np.broadcast_to(zeta[:, :, None], (RET_HEADS, t, RET_V_HEAD))
    block_decay = [float(v) for v in np.exp(t * log_g)]
    return (cos_r, sin_r, cos_b, sin_b, sign, jnp.asarray(dmask, F32), jnp.asarray(xi, F32),
            jnp.asarray(zeta, F32), block_decay)


def _swap_halves(x):
    n = x.shape[1]
    half = RET_QK_HEAD // 2
    fwd = pltpu.roll(x, half, 1)
    bwd = pltpu.roll(x, n - half, 1)
    lane = lax.broadcasted_iota(jnp.int32, x.shape, 1)
    return jnp.where((lane & (RET_QK_HEAD - 1)) < half, bwd, fwd)


def _retention_block(block_decay, q_ref, k_ref, v_ref, z_ref, cosr_ref, sinr_ref, cosb_ref, sinb_ref,
                     sign_ref, dm_ref, xi_ref, zeta_ref, o_ref, state_ref):
    cb = cosb_ref[...]
    sb = sinb_ref[...]
    cos2 = cb * cosr_ref[...] - sb * sinr_ref[...]
    sin2 = (sb * cosr_ref[...] + cb * sinr_ref[...]) * sign_ref[...]
    reps = RET_QK // cos2.shape[1]
    cos = jnp.concatenate([cos2] * reps, axis=1)
    sin = jnp.concatenate([sin2] * reps, axis=1)
    q = q_ref[...].astype(F32)
    k = k_ref[...].astype(F32)
    qr = (q * cos + _swap_halves(q) * sin).astype(BF16)
    kr = ((k * cos + _swap_halves(k) * sin) * (RET_QK_HEAD ** -0.5)).astype(BF16)
    for h in range(RET_HEADS):
        qs = slice(h * RET_QK_HEAD, (h + 1) * RET_QK_HEAD)
        vs = slice(h * RET_V_HEAD, (h + 1) * RET_V_HEAD)
        qh = qr[:, qs]
        kh = kr[:, qs]
        vh = v_ref[:, vs]
        s = lax.dot_general(qh, kh, (((1,), (1,)), ((), ())), preferred_element_type=F32)
        s = s * dm_ref[h]
        inner = _dot(s.astype(BF16), vh)
        st = state_ref[h]
        cross = _dot(qh, st.astype(BF16)) * xi_ref[h]
        vz = (vh.astype(F32) * zeta_ref[h]).astype(BF16)
        kv = lax.dot_general(kh, vz, (((0,), (0,)), ((), ())), preferred_element_type=F32)
        state_ref[h] = st * block_decay[h] + kv
        o = inner + cross
        o = o * lax.rsqrt(jnp.mean(o * o, axis=-1, keepdims=True) + EPS)
        o_ref[:, vs] = (o * jax.nn.silu(z_ref[:, vs].astype(F32))).astype(o_ref.dtype)


def _merge_block(ys_ref, yr_ref, gs0_ref, gs1_ref, gr0_ref, gr1_ref, x_ref, ws_ref, wr_ref, wo_ref,
                 g_ref, o_ref):
    a = _dot(ys_ref[...], ws_ref[...])
    b = _dot(yr_ref[...], wr_ref[...])
    gs = jnp.concatenate([gs0_ref[...], gs1_ref[...]], axis=1).astype(F32)
    gr = jnp.concatenate([gr0_ref[...], gr1_ref[...]], axis=1).astype(F32)
    m = jax.nn.sigmoid(gs) * a + jax.nn.sigmoid(gr) * b
    o = _dot(m.astype(BF16), wo_ref[...])
    o_ref[...] = x_ref[...] + _rms_rows(o, g_ref[...])


def _tail_kernel(block_decay,
                 yssm_ref, zs_ref, q_ref, k_ref, v_ref, zr_ref, cosb_ref, sinb_ref,
                 gs0_ref, gs1_ref, gr0_ref, gr1_ref, x_ref,
                 wglu_ref, bglu_ref, cosr_ref, sinr_ref, sign_ref, dm_ref, xi_ref, zeta_ref,
                 ws_ref, wr_ref, wo_ref, g_ref,
                 o_ref, state_ref, ys_scr, yr_scr):
    @pl.when(pl.program_id(0) == 0)
    def _():
        state_ref[...] = jnp.zeros_like(state_ref)
        ys_scr[...] = jnp.zeros_like(ys_scr)
        yr_scr[...] = jnp.zeros_like(yr_scr)

    _merge_block(ys_scr, yr_scr, gs0_ref, gs1_ref, gr0_ref, gr1_ref, x_ref, ws_ref, wr_ref, wo_ref,
                 g_ref, o_ref)
    _s5post_block(yssm_ref, zs_ref, wglu_ref, bglu_ref, ys_scr)
    _retention_block(block_decay, q_ref, k_ref, v_ref, zr_ref, cosr_ref, sinr_ref, cosb_ref, sinb_ref,
                     sign_ref, dm_ref, xi_ref, zeta_ref, yr_scr, state_ref)


def _tail(y_ssm, proj, x2, w_glu, b_glu, w_s, w_r, w_o, gain):
    seq, d = x2.shape
    t = RET_T
    nblk = seq // t
    cos_r, sin_r, cos_b, sin_b, sign, dmask, xi, zeta, block_decay = _ret_tables(seq)
    blk = 1024
    cur = lambda s: jnp.minimum(s, nblk - 1)
    prv = lambda s: jnp.maximum(s - 1, 0)
    single = pl.Buffered(1)
    const = lambda a: pl.BlockSpec(a.shape, lambda s: (0,) * a.ndim, pipeline_mode=single)
    b_glu2 = b_glu.reshape(1, S5_WIDTH).astype(F32)
    gain2 = gain.reshape(1, d).astype(F32)
    return pl.pallas_call(
        functools.partial(_tail_kernel, block_decay),
        grid=(nblk + 1,),
        in_specs=[
            pl.BlockSpec((t, S5_WIDTH), lambda s: (cur(s), 0)),
            pl.BlockSpec((t, S5_WIDTH), lambda s: (cur(s), _ZS_BLK)),
            pl.BlockSpec((t, RET_QK), lambda s: (cur(s), _Q_BLK512)),
            pl.BlockSpec((t, RET_QK), lambda s: (cur(s), _K_BLK512)),
            pl.BlockSpec((t, RET_WIDTH), lambda s: (cur(s), _V_BLK)),
            pl.BlockSpec((t, RET_WIDTH), lambda s: (cur(s), _ZR_BLK)),
            pl.BlockSpec((None, 1, cos_b.shape[2]), lambda s: (cur(s), 0, 0)),
            pl.BlockSpec((None, 1, sin_b.shape[2]), lambda s: (cur(s), 0, 0)),
            pl.BlockSpec((t, blk), lambda s: (prv(s), _GLS_BLK)),
            pl.BlockSpec((t, blk), lambda s: (prv(s), _GLS_BLK + 1)),
            pl.BlockSpec((t, blk), lambda s: (prv(s), _GLR_BLK)),
            pl.BlockSpec((t, blk), lambda s: (prv(s), _GLR_BLK + 1)),
            pl.BlockSpec((t, d), lambda s: (prv(s), 0)),
            const(w_glu), const(b_glu2), const(cos_r), const(sin_r), const(sign), const(dmask),
            const(xi), const(zeta), const(w_s), const(w_r), const(w_o), const(gain2),
        ],
        out_specs=pl.BlockSpec((t, d), lambda s: (prv(s), 0)),
        out_shape=jax.ShapeDtypeStruct((seq, d), F32),
        scratch_shapes=[
            pltpu.VMEM((RET_HEADS, RET_QK_HEAD, RET_V_HEAD), F32),
            pltpu.VMEM((t, S5_WIDTH), BF16),
            pltpu.VMEM((t, RET_WIDTH), BF16),
        ],
        compiler_params=pltpu.CompilerParams(
            dimension_semantics=("arbitrary",), vmem_limit_bytes=VMEM_LIMIT),
        name="tail",
    )(y_ssm, proj, proj, proj, proj, proj, cos_b, sin_b, proj, proj, proj, proj, x2,
      w_glu, b_glu2, cos_r, sin_r, sign, dmask, xi, zeta, w_s, w_r, w_o, gain2)


def kernel(x, norm_pre, w_in, lam_re, lam_im, log_step, s5_b_re, s5_b_im, s5_c_re, s5_c_im, s5_d,
           w_glu, b_glu, w_proj_s5, w_proj_ret, w_out, norm_post):
    bsz, seq, d = x.shape
    assert bsz == 1 and d == D_MODEL and seq % 1024 == 0
    depth = w_in.shape[0]
    x2 = x.reshape(seq, d)
    for l in range(depth):
        proj = _inproj(x2, norm_pre[l], w_in[l].astype(BF16))
        tables = _s5_tables(lam_re[l], lam_im[l], log_step[l], s5_b_re[l], s5_b_im[l],
                            s5_c_re[l], s5_c_im[l], s5_d[l], seq // S5_T)
        y_ssm = _s5core(proj, tables)
        x2 = _tail(y_ssm, proj, x2, w_glu[l].astype(BF16), b_glu[l], w_proj_s5[l].astype(BF16),
                   w_proj_ret[l].astype(BF16), w_out[l].astype(BF16), norm_post[l])
    return x2.reshape(bsz, seq, d)
```

```python
import functools
import math

import numpy as np
import jax
import jax.numpy as jnp
from jax import lax
from jax.experimental import pallas as pl
from jax.experimental.pallas import tpu as pltpu

F32 = jnp.float32
BF16 = jnp.bfloat16
HIGHEST = lax.Precision.HIGHEST

D_MODEL = 2048
EPS = 1e-6
CHUNK = 64

S5_WIDTH = 1024
S5_GROUP = 16
S5_GROUPS = 64
S5_STATE = 64

RET_HEADS = 8
RET_V_HEAD = 128
RET_QK_HEAD = 64
RET_QK = 512
RET_WIDTH = 1024
ROPE_BASE = 10000.0

_Q_BLK512 = (2 * S5_WIDTH) // 512
_K_BLK512 = (2 * S5_WIDTH + RET_QK) // 512
_ZS_BLK = 1
_V_BLK = (2 * S5_WIDTH + 2 * RET_QK) // 1024
_ZR_BLK = _V_BLK + 1
_GLS_BLK = _ZR_BLK + 1
_GLR_BLK = _GLS_BLK + 2

VMEM_LIMIT = 56 * 1024 * 1024
INPROJ_VMEM_LIMIT = 60 * 1024 * 1024
INPROJ_PIECE = 1280
SUBLANES = 8
LANES = 128

S5_T = 32
S5_GB = 8
S5_TW = S5_T * S5_GROUP
S5_RB = 1024
S5_SCAN_ROWS = 16
S5_BATCH = 4

RET_T = 256


def _dot(a, b):
    return jnp.dot(a, b, preferred_element_type=F32)


def _rms_rows(x, gain):
    ms = jnp.mean(x * x, axis=-1, keepdims=True)
    return x * lax.rsqrt(ms + EPS) * gain


def _inproj_kernel(n_extra, x_ref, g_ref, w_ref, *rest):
    extra_in = rest[:n_extra]
    o_ref = rest[n_extra]
    extra_out = rest[n_extra + 1:2 * n_extra + 1]
    h_ref = rest[2 * n_extra + 1]

    @pl.when(pl.program_id(1) == 0)
    def _():
        h_ref[...] = _rms_rows(x_ref[...], g_ref[...]).astype(BF16)

    tn = o_ref.shape[1]
    for lo in range(0, tn, INPROJ_PIECE):
        hi = min(lo + INPROJ_PIECE, tn)
        o_ref[:, lo:hi] = _dot(h_ref[...], w_ref[:, lo:hi]).astype(o_ref.dtype)
    for src, dst in zip(extra_in, extra_out):
        dst[...] = src[...].astype(dst.dtype)


def _inproj(x2, gain, w_bf16, later_weights, tm=1024, tn=2304):
    seq, d = x2.shape
    n = w_bf16.shape[1]
    grid = (seq // tm, n // tn)
    steps = grid[0] * grid[1]
    step_block = lambda i, j: (i * grid[1] + j, 0)
    extra_specs = [pl.BlockSpec((w.shape[0] // steps, w.shape[1]), step_block) for w in later_weights]
    outs = pl.pallas_call(
        functools.partial(_inproj_kernel, len(later_weights)),
        grid=grid,
        in_specs=[
            pl.BlockSpec((tm, d), lambda i, j: (i, 0)),
            pl.BlockSpec((1, d), lambda i, j: (0, 0)),
            pl.BlockSpec((d, tn), lambda i, j: (0, j)),
        ] + extra_specs,
        out_specs=[pl.BlockSpec((tm, tn), lambda i, j: (i, j))] + extra_specs,
        out_shape=[jax.ShapeDtypeStruct((seq, n), BF16)]
        + [jax.ShapeDtypeStruct(w.shape, BF16) for w in later_weights],
        scratch_shapes=[pltpu.VMEM((tm, d), BF16)],
        compiler_params=pltpu.CompilerParams(
            dimension_semantics=("arbitrary", "arbitrary"), vmem_limit_bytes=INPROJ_VMEM_LIMIT),
        name="inproj",
    )(x2, gain.reshape(1, d).astype(F32), w_bf16, *later_weights)
    return outs[0], outs[1:]


def _s5_tables(lam_re, lam_im, log_step, b_re, b_im, c_re, c_im, d_skip, n_chunks):
    t = S5_T
    lr = lam_re.astype(F32)
    li = lam_im.astype(F32)
    step = jnp.exp(log_step.astype(F32))[:, None]
    lrs = lr * step
    ang = li * step
    mag = jnp.exp(lrs)
    ab_re = mag * jnp.cos(ang)
    ab_im = mag * jnp.sin(ang)
    den = lr * lr + li * li
    nr = ab_re - 1.0
    f_re = (nr * lr + ab_im * li) / den
    f_im = (ab_im * lr - nr * li) / den
    br = b_re.astype(F32)
    bi = b_im.astype(F32)
    bb_re = f_re[..., None] * br - f_im[..., None] * bi
    bb_im = f_re[..., None] * bi + f_im[..., None] * br
    cr = c_re.astype(F32)
    ci = c_im.astype(F32)

    n_bits = int(math.log2(t))
    bit_re, bit_im = [ab_re], [ab_im]
    for _ in range(n_bits - 1):
        r, i = bit_re[-1], bit_im[-1]
        bit_re.append(r * r - i * i)
        bit_im.append(2.0 * r * i)
    pad_bits = ((0, 0), (0, 0), (0, SUBLANES - n_bits))
    abit_re = jnp.pad(jnp.stack(bit_re, axis=-1), pad_bits)
    abit_im = jnp.pad(jnp.stack(bit_im, axis=-1), pad_bits)
    bbt_re = jnp.tile(bb_re, (1, 1, LANES // S5_GROUP))
    bbt_im = jnp.tile(bb_im, (1, 1, LANES // S5_GROUP))
    e_ip = (jnp.arange(t, dtype=F32) + 1.0)[None, :, None]
    m_ip = jnp.exp(e_ip * lrs[:, None, :])
    pr = m_ip * jnp.cos(e_ip * ang[:, None, :])
    pi = m_ip * jnp.sin(e_ip * ang[:, None, :])
    p1 = jnp.concatenate([pr, -pi], axis=2)
    p2 = jnp.concatenate([-pi, -pr], axis=2)
    c_rr = jnp.concatenate([cr, cr], axis=2)
    c_ii = jnp.concatenate([ci, ci], axis=2)

    c_cat = jnp.concatenate([cr, -ci], axis=2)
    skip = jnp.broadcast_to(d_skip.astype(F32).reshape(S5_GROUPS, S5_GROUP, 1),
                            (S5_GROUPS, S5_GROUP, LANES))

    n_steps = max(1, int(math.ceil(math.log2(n_chunks))))
    m_t = jnp.exp(float(t) * lrs)
    sq_re, sq_im = [m_t * jnp.cos(float(t) * ang)], [m_t * jnp.sin(float(t) * ang)]
    for _ in range(n_steps - 1):
        r, i = sq_re[-1], sq_im[-1]
        sq_re.append(r * r - i * i)
        sq_im.append(2.0 * r * i)
    def pair_rows(sq):
        a = jnp.stack(sq, axis=0).reshape(n_steps, S5_GROUPS // 2, 2 * S5_STATE)
        a = jnp.transpose(a, (1, 0, 2))
        return jnp.pad(a, ((0, 0), (0, S5_SCAN_ROWS - n_steps), (0, 0)))

    per_group = (abit_re, abit_im, bbt_re, bbt_im, p1, p2, c_rr, c_ii, c_cat, skip)
    return per_group, pair_rows(sq_re), pair_rows(sq_im), n_steps


def _sublane_transpose(v):
    v = list(v)
    sub = lax.broadcasted_iota(jnp.int32, v[0].shape, 1)
    for k in (4, 2, 1):
        keep = (sub & k) == 0
        for r in range(SUBLANES):
            if r & k:
                continue
            a, b = v[r], v[r | k]
            v[r] = jnp.where(keep, a, pltpu.roll(b, k, 1))
            v[r | k] = jnp.where(keep, pltpu.roll(a, SUBLANES - k, 1), b)
    return v


def _s5core_kernel(n_steps, u_ref, abr_ref, abi_ref, bbr_ref, bbi_ref, p1_ref, p2_ref, crr_ref, cii_ref,
                   ccat_ref, skip_ref, apr_ref, api_ref, o_ref, step_scr, rhs_scr, mt_scr, yt_scr):
    t = S5_T
    nch = step_scr.shape[1]
    nq = t // SUBLANES
    nb = S5_RB // t
    nsub = nb // SUBLANES

    def deinterleave(blk, carry):
        rows = pl.ds(pl.multiple_of(blk * S5_RB, S5_RB), S5_RB)
        x = u_ref[rows, :].astype(F32).reshape(nsub, SUBLANES, nq, SUBLANES, LANES)
        dst = pl.ds(pl.multiple_of(blk * nb, nb), nb)
        for q in range(nq):
            w = _sublane_transpose([x[:, r, q] for r in range(SUBLANES)])
            for s in range(SUBLANES):
                step_scr[q * SUBLANES + s, dst, :] = w[s].reshape(nb, LANES)
        return carry

    lax.fori_loop(0, (nch * t) // S5_RB, deinterleave, 0)

    for i in range(t):
        rhs_scr[:, i * S5_GROUP:(i + 1) * S5_GROUP, :] = (
            step_scr[i].T.astype(BF16).reshape(S5_GB, S5_GROUP, nch))

    tail_lane = lax.broadcasted_iota(jnp.int32, (S5_GROUP, LANES), 1)
    tail_row = lax.broadcasted_iota(jnp.int32, (S5_GROUP, LANES), 0)
    lag0 = tail_lane == tail_row + (LANES - S5_GROUP)
    zero_half = jnp.zeros((S5_GROUP, S5_TW), F32)
    per_tile = LANES // S5_GROUP
    chunk_row = lax.broadcasted_iota(jnp.int32, (nch, LANES), 0)

    def shift_down(v, k):
        if k % SUBLANES == 0:
            return jnp.concatenate([jnp.zeros((k, LANES), F32), v[:nch - k]], axis=0)
        return jnp.where(chunk_row >= k, pltpu.roll(v, k, 0), 0.0)

    lane_expo = (t - 1) - lax.shift_right_logical(
        lax.broadcasted_iota(jnp.int32, (S5_STATE, S5_TW), 1), int(math.log2(S5_GROUP)))
    lane_reps = S5_TW // LANES

    def state_weights(g):
        abr = abr_ref[g]
        abi = abi_ref[g]
        pw_re = pw_im = None
        for b in range(int(math.log2(t))):
            bit = (lax.shift_right_logical(lane_expo, b) & 1) == 1
            f_re = jnp.where(bit, abr[:, b:b + 1], 1.0)
            f_im = jnp.where(bit, abi[:, b:b + 1], 0.0)
            if pw_re is None:
                pw_re, pw_im = f_re, f_im
            else:
                pw_re, pw_im = pw_re * f_re - pw_im * f_im, pw_re * f_im + pw_im * f_re
        br = jnp.concatenate([bbr_ref[g]] * lane_reps, axis=1)
        bi = jnp.concatenate([bbi_ref[g]] * lane_reps, axis=1)
        return jnp.concatenate([pw_re * br - pw_im * bi, pw_re * bi + pw_im * br], axis=0)

    def output_weights(g):
        crr = crr_ref[g]
        cii = cii_ref[g]
        p1 = p1_ref[g]
        p2 = p2_ref[g]
        return jnp.concatenate(
            [(crr * p1[i:i + 1, :] + cii * p2[i:i + 1, :]).astype(BF16) for i in range(t)], axis=0)

    def local_products(g, slot):
        mt = mt_scr.at[slot]
        ws = state_weights(g)
        strip = jnp.dot(ccat_ref[g], ws, precision=HIGHEST, preferred_element_type=F32)
        tail = strip[:, S5_TW - LANES:] + jnp.where(lag0, skip_ref[g], 0.0)
        full = jnp.concatenate([strip[:, :S5_TW - LANES], tail, zero_half], axis=1)
        rot = [full] + [pltpu.roll(full, 2 * S5_TW - S5_GROUP * r, 1) for r in range(1, per_tile)]
        for i in range(t):
            a, r = divmod(t - 1 - i, per_tile)
            mt[i * S5_GROUP:(i + 1) * S5_GROUP, :] = (
                rot[r][:, a * LANES:a * LANES + S5_TW].astype(BF16))
        rhs = rhs_scr[g]
        yt_scr[g] = _dot(mt[...], rhs)
        return _dot(ws.astype(BF16), rhs)

    def scan_pair(gp, g0, s0, s1):
        xr = jnp.concatenate([s0[:S5_STATE], s1[:S5_STATE]], axis=0).T
        xi = jnp.concatenate([s0[S5_STATE:], s1[S5_STATE:]], axis=0).T
        apr = apr_ref[gp]
        api = api_ref[gp]
        for k in range(n_steps):
            ar = apr[k:k + 1, :]
            ai = api[k:k + 1, :]
            sr = shift_down(xr, 1 << k)
            si = shift_down(xi, 1 << k)
            xr, xi = xr + (ar * sr - ai * si), xi + (ar * si + ai * sr)
        pr = shift_down(xr, 1).T
        pi = shift_down(xi, 1).T
        for j in range(2):
            rows = slice(j * S5_STATE, (j + 1) * S5_STATE)
            xprev = jnp.concatenate([pr[rows], pi[rows]], axis=0).astype(BF16)
            yt_scr[g0 + j] += _dot(output_weights(g0 + j), xprev)

    def per_batch(b, carry):
        g0 = S5_BATCH * b
        s = [local_products(g0 + j, j) for j in range(S5_BATCH)]
        for p in range(S5_BATCH // 2):
            scan_pair((S5_BATCH // 2) * b + p, g0 + 2 * p, s[2 * p], s[2 * p + 1])
        return carry

    lax.fori_loop(0, S5_GB // S5_BATCH, per_batch, 0)

    for i in range(t):
        blk = yt_scr[:, i * S5_GROUP:(i + 1) * S5_GROUP, :].reshape(S5_GB * S5_GROUP, nch)
        step_scr[i] = blk.T

    def interleave(blk, carry):
        src = pl.ds(pl.multiple_of(blk * nb, nb), nb)
        per_q = []
        for q in range(nq):
            w = [step_scr[q * SUBLANES + s, src, :].reshape(nsub, SUBLANES, LANES)
                 for s in range(SUBLANES)]
            per_q.append(_sublane_transpose(w))
        x = jnp.stack([jnp.stack([per_q[q][r] for q in range(nq)], axis=1)
                       for r in range(SUBLANES)], axis=1)
        rows = pl.ds(pl.multiple_of(blk * S5_RB, S5_RB), S5_RB)
        o_ref[rows, :] = x.reshape(S5_RB, LANES).astype(o_ref.dtype)
        return carry

    lax.fori_loop(0, (nch * t) // S5_RB, interleave, 0)


def _s5core(proj, tables):
    seq = proj.shape[0]
    per_group, ap_re, ap_im, n_steps = tables
    nch = seq // S5_T
    lanes = S5_GB * S5_GROUP
    grp = lambda i: (i, 0, 0)
    gspec = lambda a: pl.BlockSpec((S5_GB,) + a.shape[1:], grp)
    pspec = lambda a: pl.BlockSpec((S5_GB // 2,) + a.shape[1:], grp)
    return pl.pallas_call(
        functools.partial(_s5core_kernel, n_steps),
        grid=(S5_GROUPS // S5_GB,),
        in_specs=([pl.BlockSpec((seq, lanes), lambda i: (0, i))] + [gspec(a) for a in per_group]
                  + [pspec(ap_re), pspec(ap_im)]),
        out_specs=pl.BlockSpec((seq, lanes), lambda i: (0, i)),
        out_shape=jax.ShapeDtypeStruct((seq, S5_WIDTH), BF16),
        scratch_shapes=[
            pltpu.VMEM((S5_T, nch, lanes), F32),
            pltpu.VMEM((S5_GB, S5_TW, nch), BF16),
            pltpu.VMEM((S5_BATCH, S5_TW, S5_TW), BF16),
            pltpu.VMEM((S5_GB, S5_TW, nch), F32),
        ],
        compiler_params=pltpu.CompilerParams(
            dimension_semantics=("parallel",), vmem_limit_bytes=VMEM_LIMIT),
        name="s5_core",
    )(proj, *per_group, ap_re, ap_im)


def _gelu_tanh(y):
    return 0.5 * y * (1.0 + jnp.tanh(math.sqrt(2.0 / math.pi) * (y + 0.044715 * (y * y * y))))


def _s5post_block(y_ref, z_ref, wglu_ref, bglu_ref, o_ref):
    g = _gelu_tanh(y_ref[...].astype(F32))
    gate = jax.nn.sigmoid(_dot(g.astype(BF16), wglu_ref[...]) + bglu_ref[...])
    o_ref[...] = (g * gate * jax.nn.silu(z_ref[...].astype(F32))).astype(o_ref.dtype)


def _ret_tables(seq):
    t = RET_T
    inv = ROPE_BASE ** (-jnp.arange(0, RET_QK_HEAD, 2, dtype=F32) / RET_QK_HEAD)
    tile4 = lambda a: jnp.concatenate([a, a, a, a], axis=-1)
    ang_r = jnp.arange(t, dtype=F32)[:, None] * inv[None, :]
    ang_b = (jnp.arange(seq // t, dtype=F32) * float(t))[:, None] * inv[None, :]
    cos_r, sin_r = tile4(jnp.cos(ang_r)), tile4(jnp.sin(ang_r))
    cos_b, sin_b = tile4(jnp.cos(ang_b))[:, None, :], tile4(jnp.sin(ang_b))[:, None, :]
    half = RET_QK_HEAD // 2
    sign = jnp.asarray(np.tile(np.repeat([-1.0, 1.0], half), 2)[None, :], F32)

    log_g = np.log1p(-np.exp2(-5.0 - np.arange(RET_HEADS, dtype=np.float64)))
    i = np.arange(t)
    ci = i // CHUNK
    diff = (i[:, None] - i[None, :]).astype(np.float64)
    same = ci[:, None] == ci[None, :]
    earlier = ci[None, :] < ci[:, None]
    expo = np.where(same, np.abs(diff), diff)
    dmask = np.where((same | earlier)[None], np.exp(expo[None] * log_g[:, None, None]), 0.0)
    xi = np.exp((i + 1.0)[None, :] * log_g[:, None])
    zeta = np.exp((t - 1.0 - i)[None, :] * log_g[:, None])
    xi = np.broadcast_to(xi[:, :, None], (RET_HEADS, t, RET_V_HEAD))
    zeta = np.broadcast_to(zeta[:, :, None], (RET_HEADS, t, RET_V_HEAD))
    block_decay = [float(v) for v in np.exp(t * log_g)]
    return (cos_r, sin_r, cos_b, sin_b, sign, jnp.asarray(dmask, F32), jnp.asarray(xi, F32),
            jnp.asarray(zeta, F32), block_decay)


def _swap_halves(x):
    n = x.shape[1]
    half = RET_QK_HEAD // 2
    fwd = pltpu.roll(x, half, 1)
    bwd = pltpu.roll(x, n - half, 1)
    lane = lax.broadcasted_iota(jnp.int32, x.shape, 1)
    return jnp.where((lane & (RET_QK_HEAD - 1)) < half, bwd, fwd)


def _retention_block(block_decay, q_ref, k_ref, v_ref, z_ref, cosr_ref, sinr_ref, cosb_ref, sinb_ref,
                     sign_ref, dm_ref, xi_ref, zeta_ref, o_ref, state_ref):
    cb = cosb_ref[...]
    sb = sinb_ref[...]
    cos2 = cb * cosr_ref[...] - sb * sinr_ref[...]
    sin2 = (sb * cosr_ref[...] + cb * sinr_ref[...]) * sign_ref[...]
    reps = RET_QK // cos2.shape[1]
    cos = jnp.concatenate([cos2] * reps, axis=1)
    sin = jnp.concatenate([sin2] * reps, axis=1)
    q = q_ref[...].astype(F32)
    k = k_ref[...].astype(F32)
    qr = (q * cos + _swap_halves(q) * sin).astype(BF16)
    kr = ((k * cos + _swap_halves(k) * sin) * (RET_QK_HEAD ** -0.5)).astype(BF16)
    for h in range(RET_HEADS):
        qs = slice(h * RET_QK_HEAD, (h + 1) * RET_QK_HEAD)
        vs = slice(h * RET_V_HEAD, (h + 1) * RET_V_HEAD)
        qh = qr[:, qs]
        kh = kr[:, qs]
        vh = v_ref[:, vs]
        s = lax.dot_general(qh, kh, (((1,), (1,)), ((), ())), preferred_element_type=F32)
        s = s * dm_ref[h]
        inner = _dot(s.astype(BF16), vh)
        st = state_ref[h]
        cross = _dot(qh, st.astype(BF16)) * xi_ref[h]
        vz = (vh.astype(F32) * zeta_ref[h]).astype(BF16)
        kv = lax.dot_general(kh, vz, (((0,), (0,)), ((), ())), preferred_element_type=F32)
        state_ref[h] = st * block_decay[h] + kv
        o = inner + cross
        o = o * lax.rsqrt(jnp.mean(o * o, axis=-1, keepdims=True) + EPS)
        o_ref[:, vs] = (o * jax.nn.silu(z_ref[:, vs].astype(F32))).astype(o_ref.dtype)


def _merge_block(ys_ref, yr_ref, gs0_ref, gs1_ref, gr0_ref, gr1_ref, x_ref, ws_ref, wr_ref, wo_ref,
                 g_ref, o_ref):
    a = _dot(ys_ref[...], ws_ref[...])
    b = _dot(yr_ref[...], wr_ref[...])
    gs = jnp.concatenate([gs0_ref[...], gs1_ref[...]], axis=1).astype(F32)
    gr = jnp.concatenate([gr0_ref[...], gr1_ref[...]], axis=1).astype(F32)
    m = jax.nn.sigmoid(gs) * a + jax.nn.sigmoid(gr) * b
    o = _dot(m.astype(BF16), wo_ref[...])
    o_ref[...] = x_ref[...] + _rms_rows(o, g_ref[...])


def _tail_kernel(block_decay,
                 yssm_ref, zs_ref, q_ref, k_ref, v_ref, zr_ref, cosb_ref, sinb_ref,
                 gs0_ref, gs1_ref, gr0_ref, gr1_ref, x_ref,
                 wglu_ref, bglu_ref, cosr_ref, sinr_ref, sign_ref, dm_ref, xi_ref, zeta_ref,
                 ws_ref, wr_ref, wo_ref, g_ref,
                 o_ref, state_ref, ys_scr, yr_scr):
    @pl.when(pl.program_id(0) == 0)
    def _():
        state_ref[...] = jnp.zeros_like(state_ref)
        ys_scr[...] = jnp.zeros_like(ys_scr)
        yr_scr[...] = jnp.zeros_like(yr_scr)

    _merge_block(ys_scr, yr_scr, gs0_ref, gs1_ref, gr0_ref, gr1_ref, x_ref, ws_ref, wr_ref, wo_ref,
                 g_ref, o_ref)
    _s5post_block(yssm_ref, zs_ref, wglu_ref, bglu_ref, ys_scr)
    _retention_block(block_decay, q_ref, k_ref, v_ref, zr_ref, cosr_ref, sinr_ref, cosb_ref, sinb_ref,
                     sign_ref, dm_ref, xi_ref, zeta_ref, yr_scr, state_ref)


def _tail(y_ssm, proj, x2, w_glu, b_glu, w_s, w_r, w_o, gain):
    seq, d = x2.shape
    t = RET_T
    nblk = seq // t
    cos_r, sin_r, cos_b, sin_b, sign, dmask, xi, zeta, block_decay = _ret_tables(seq)
    blk = 1024
    cur = lambda s: jnp.minimum(s, nblk - 1)
    prv = lambda s: jnp.maximum(s - 1, 0)
    single = pl.Buffered(1)
    const = lambda a: pl.BlockSpec(a.shape, lambda s: (0,) * a.ndim, pipeline_mode=single)
    b_glu2 = b_glu.reshape(1, S5_WIDTH).astype(F32)
    gain2 = gain.reshape(1, d).astype(F32)
    return pl.pallas_call(
        functools.partial(_tail_kernel, block_decay),
        grid=(nblk + 1,),
        in_specs=[
            pl.BlockSpec((t, S5_WIDTH), lambda s: (cur(s), 0)),
            pl.BlockSpec((t, S5_WIDTH), lambda s: (cur(s), _ZS_BLK)),
            pl.BlockSpec((t, RET_QK), lambda s: (cur(s), _Q_BLK512)),
            pl.BlockSpec((t, RET_QK), lambda s: (cur(s), _K_BLK512)),
            pl.BlockSpec((t, RET_WIDTH), lambda s: (cur(s), _V_BLK)),
            pl.BlockSpec((t, RET_WIDTH), lambda s: (cur(s), _ZR_BLK)),
            pl.BlockSpec((None, 1, cos_b.shape[2]), lambda s: (cur(s), 0, 0)),
            pl.BlockSpec((None, 1, sin_b.shape[2]), lambda s: (cur(s), 0, 0)),
            pl.BlockSpec((t, blk), lambda s: (prv(s), _GLS_BLK)),
            pl.BlockSpec((t, blk), lambda s: (prv(s), _GLS_BLK + 1)),
            pl.BlockSpec((t, blk), lambda s: (prv(s), _GLR_BLK)),
            pl.BlockSpec((t, blk), lambda s: (prv(s), _GLR_BLK + 1)),
            pl.BlockSpec((t, d), lambda s: (prv(s), 0)),
            const(w_glu), const(b_glu2), const(cos_r), const(sin_r), const(sign), const(dmask),
            const(xi), const(zeta), const(w_s), const(w_r), const(w_o), const(gain2),
        ],
        out_specs=pl.BlockSpec((t, d), lambda s: (prv(s), 0)),
        out_shape=jax.ShapeDtypeStruct((seq, d), F32),
        scratch_shapes=[
            pltpu.VMEM((RET_HEADS, RET_QK_HEAD, RET_V_HEAD), F32),
            pltpu.VMEM((t, S5_WIDTH), BF16),
            pltpu.VMEM((t, RET_WIDTH), BF16),
        ],
        compiler_params=pltpu.CompilerParams(
            dimension_semantics=("arbitrary",), vmem_limit_bytes=VMEM_LIMIT),
        name="tail",
    )(y_ssm, proj, proj, proj, proj, proj, cos_b, sin_b, proj, proj, proj, proj, x2,
      w_glu, b_glu2, cos_r, sin_r, sign, dmask, xi, zeta, w_s, w_r, w_o, gain2)


def kernel(x, norm_pre, w_in, lam_re, lam_im, log_step, s5_b_re, s5_b_im, s5_c_re, s5_c_im, s5_d,
           w_glu, b_glu, w_proj_s5, w_proj_ret, w_out, norm_post):
    bsz, seq, d = x.shape
    assert bsz == 1 and d == D_MODEL and seq % 1024 == 0
    depth = w_in.shape[0]
    x2 = x.reshape(seq, d)
    for l in range(depth):
        proj, (w_glu_b, w_s_b, w_r_b, w_o_b) = _inproj(
            x2, norm_pre[l], w_in[l].astype(BF16), (w_glu[l], w_proj_s5[l], w_proj_ret[l], w_out[l]))
        tables = _s5_tables(lam_re[l], lam_im[l], log_step[l], s5_b_re[l], s5_b_im[l],
                            s5_c_re[l], s5_c_im[l], s5_d[l], seq // S5_T)
        y_ssm = _s5core(proj, tables)
        x2 = _tail(y_ssm, proj, x2, w_glu_b, b_glu[l], w_s_b, w_r_b, w_o_b, norm_post[l])
    return x2.reshape(bsz, seq, d)
```

```python
import functools
import math

import numpy as np
import jax
import jax.numpy as jnp
from jax import lax
from jax.experimental import pallas as pl
from jax.experimental.pallas import tpu as pltpu

F32 = jnp.float32
BF16 = jnp.bfloat16
HIGHEST = lax.Precision.HIGHEST

D_MODEL = 2048
EPS = 1e-6
CHUNK = 64

S5_WIDTH = 1024
S5_GROUP = 16
S5_GROUPS = 64
S5_STATE = 64

RET_HEADS = 8
RET_V_HEAD = 128
RET_QK_HEAD = 64
RET_QK = 512
RET_WIDTH = 1024
ROPE_BASE = 10000.0

_Q_BLK512 = (2 * S5_WIDTH) // 512
_K_BLK512 = (2 * S5_WIDTH + RET_QK) // 512
_ZS_BLK = 1
_V_BLK = (2 * S5_WIDTH + 2 * RET_QK) // 1024
_ZR_BLK = _V_BLK + 1
_GLS_BLK = _ZR_BLK + 1
_GLR_BLK = _GLS_BLK + 2

VMEM_LIMIT = 56 * 1024 * 1024
INPROJ_VMEM_LIMIT = 60 * 1024 * 1024
INPROJ_PIECE = 1280
SUBLANES = 8
LANES = 128

S5_T = 32
S5_GB = 8
S5_TW = S5_T * S5_GROUP
S5_RB = 4096
S5_SCAN_ROWS = 16
S5_BATCH = 4

RET_T = 256


def _dot(a, b):
    return jnp.dot(a, b, preferred_element_type=F32)


def _rms_rows(x, gain):
    ms = jnp.mean(x * x, axis=-1, keepdims=True)
    return x * lax.rsqrt(ms + EPS) * gain


def _inproj_kernel(n_extra, x_ref, g_ref, w_ref, *rest):
    extra_in = rest[:n_extra]
    o_ref = rest[n_extra]
    extra_out = rest[n_extra + 1:2 * n_extra + 1]
    h_ref = rest[2 * n_extra + 1]

    @pl.when(pl.program_id(1) == 0)
    def _():
        h_ref[...] = _rms_rows(x_ref[...], g_ref[...]).astype(BF16)

    tn = o_ref.shape[1]
    for lo in range(0, tn, INPROJ_PIECE):
        hi = min(lo + INPROJ_PIECE, tn)
        o_ref[:, lo:hi] = _dot(h_ref[...], w_ref[:, lo:hi]).astype(o_ref.dtype)
    for src, dst in zip(extra_in, extra_out):
        dst[...] = src[...].astype(dst.dtype)


def _inproj(x2, gain, w_bf16, later_weights, tm=1024, tn=2304):
    seq, d = x2.shape
    n = w_bf16.shape[1]
    grid = (seq // tm, n // tn)
    steps = grid[0] * grid[1]
    step_block = lambda i, j: (i * grid[1] + j, 0)
    extra_specs = [pl.BlockSpec((w.shape[0] // steps, w.shape[1]), step_block) for w in later_weights]
    outs = pl.pallas_call(
        functools.partial(_inproj_kernel, len(later_weights)),
        grid=grid,
        in_specs=[
            pl.BlockSpec((tm, d), lambda i, j: (i, 0)),
            pl.BlockSpec((1, d), lambda i, j: (0, 0)),
            pl.BlockSpec((d, tn), lambda i, j: (0, j)),
        ] + extra_specs,
        out_specs=[pl.BlockSpec((tm, tn), lambda i, j: (i, j))] + extra_specs,
        out_shape=[jax.ShapeDtypeStruct((seq, n), BF16)]
        + [jax.ShapeDtypeStruct(w.shape, BF16) for w in later_weights],
        scratch_shapes=[pltpu.VMEM((tm, d), BF16)],
        compiler_params=pltpu.CompilerParams(
            dimension_semantics=("arbitrary", "arbitrary"), vmem_limit_bytes=INPROJ_VMEM_LIMIT),
        name="inproj",
    )(x2, gain.reshape(1, d).astype(F32), w_bf16, *later_weights)
    return outs[0], outs[1:]


def _s5_tables(lam_re, lam_im, log_step, b_re, b_im, c_re, c_im, d_skip, n_chunks):
    t = S5_T
    lr = lam_re.astype(F32)
    li = lam_im.astype(F32)
    step = jnp.exp(log_step.astype(F32))[:, None]
    lrs = lr * step
    ang = li * step
    mag = jnp.exp(lrs)
    ab_re = mag * jnp.cos(ang)
    ab_im = mag * jnp.sin(ang)
    den = lr * lr + li * li
    nr = ab_re - 1.0
    f_re = (nr * lr + ab_im * li) / den
    f_im = (ab_im * lr - nr * li) / den
    br = b_re.astype(F32)
    bi = b_im.astype(F32)
    bb_re = f_re[..., None] * br - f_im[..., None] * bi
    bb_im = f_re[..., None] * bi + f_im[..., None] * br
    cr = c_re.astype(F32)
    ci = c_im.astype(F32)

    n_bits = int(math.log2(t))
    bit_re, bit_im = [ab_re], [ab_im]
    for _ in range(n_bits - 1):
        r, i = bit_re[-1], bit_im[-1]
        bit_re.append(r * r - i * i)
        bit_im.append(2.0 * r * i)
    pad_bits = ((0, 0), (0, 0), (0, SUBLANES - n_bits))
    abit_re = jnp.pad(jnp.stack(bit_re, axis=-1), pad_bits)
    abit_im = jnp.pad(jnp.stack(bit_im, axis=-1), pad_bits)
    bbt_re = jnp.tile(bb_re, (1, 1, LANES // S5_GROUP))
    bbt_im = jnp.tile(bb_im, (1, 1, LANES // S5_GROUP))
    e_ip = (jnp.arange(t, dtype=F32) + 1.0)[None, :, None]
    m_ip = jnp.exp(e_ip * lrs[:, None, :])
    pr = m_ip * jnp.cos(e_ip * ang[:, None, :])
    pi = m_ip * jnp.sin(e_ip * ang[:, None, :])
    p1 = jnp.concatenate([pr, -pi], axis=2)
    p2 = jnp.concatenate([-pi, -pr], axis=2)
    c_rr = jnp.concatenate([cr, cr], axis=2)
    c_ii = jnp.concatenate([ci, ci], axis=2)

    c_cat = jnp.concatenate([cr, -ci], axis=2)
    skip = jnp.broadcast_to(d_skip.astype(F32).reshape(S5_GROUPS, S5_GROUP, 1),
                            (S5_GROUPS, S5_GROUP, LANES))

    n_steps = max(1, int(math.ceil(math.log2(n_chunks))))
    m_t = jnp.exp(float(t) * lrs)
    sq_re, sq_im = [m_t * jnp.cos(float(t) * ang)], [m_t * jnp.sin(float(t) * ang)]
    for _ in range(n_steps - 1):
        r, i = sq_re[-1], sq_im[-1]
        sq_re.append(r * r - i * i)
        sq_im.append(2.0 * r * i)
    def pair_rows(sq):
        a = jnp.stack(sq, axis=0).reshape(n_steps, S5_GROUPS // 2, 2 * S5_STATE)
        a = jnp.transpose(a, (1, 0, 2))
        return jnp.pad(a, ((0, 0), (0, S5_SCAN_ROWS - n_steps), (0, 0)))

    per_group = (abit_re, abit_im, bbt_re, bbt_im, p1, p2, c_rr, c_ii, c_cat, skip)
    return per_group, pair_rows(sq_re), pair_rows(sq_im), n_steps


def _sublane_transpose(v):
    v = list(v)
    sub = lax.broadcasted_iota(jnp.int32, v[0].shape, 1)
    for k in (4, 2, 1):
        keep = (sub & k) == 0
        for r in range(SUBLANES):
            if r & k:
                continue
            a, b = v[r], v[r | k]
            v[r] = jnp.where(keep, a, pltpu.roll(b, k, 1))
            v[r | k] = jnp.where(keep, pltpu.roll(a, SUBLANES - k, 1), b)
    return v


def _s5core_kernel(n_steps, u_ref, abr_ref, abi_ref, bbr_ref, bbi_ref, p1_ref, p2_ref, crr_ref, cii_ref,
                   ccat_ref, skip_ref, apr_ref, api_ref, o_ref, rhs_scr, mt_scr, yt_scr):
    t = S5_T
    nch = rhs_scr.shape[2]
    nq = t // SUBLANES
    rb = min(S5_RB, nch * t)
    nb = rb // t
    nsub = nb // SUBLANES
    n_blocks = (nch * t) // rb

    for blk in range(n_blocks):
        x = u_ref[blk * rb:(blk + 1) * rb, :].astype(F32).reshape(nsub, SUBLANES, nq, SUBLANES, LANES)
        cols = slice(blk * nb, (blk + 1) * nb)
        for q in range(nq):
            w = _sublane_transpose([x[:, r, q] for r in range(SUBLANES)])
            for s in range(SUBLANES):
                i = q * SUBLANES + s
                rhs_scr[:, i * S5_GROUP:(i + 1) * S5_GROUP, cols] = (
                    w[s].reshape(nb, LANES).T.astype(BF16).reshape(S5_GB, S5_GROUP, nb))

    tail_lane = lax.broadcasted_iota(jnp.int32, (S5_GROUP, LANES), 1)
    tail_row = lax.broadcasted_iota(jnp.int32, (S5_GROUP, LANES), 0)
    lag0 = tail_lane == tail_row + (LANES - S5_GROUP)
    zero_half = jnp.zeros((S5_GROUP, S5_TW), F32)
    per_tile = LANES // S5_GROUP
    chunk_row = lax.broadcasted_iota(jnp.int32, (nch, LANES), 0)

    def shift_down(v, k):
        if k % SUBLANES == 0:
            return jnp.concatenate([jnp.zeros((k, LANES), F32), v[:nch - k]], axis=0)
        return jnp.where(chunk_row >= k, pltpu.roll(v, k, 0), 0.0)

    lane_expo = (t - 1) - lax.shift_right_logical(
        lax.broadcasted_iota(jnp.int32, (S5_STATE, S5_TW), 1), int(math.log2(S5_GROUP)))
    lane_reps = S5_TW // LANES

    def state_weights(g):
        abr = abr_ref[g]
        abi = abi_ref[g]
        pw_re = pw_im = None
        for b in range(int(math.log2(t))):
            bit = (lax.shift_right_logical(lane_expo, b) & 1) == 1
            f_re = jnp.where(bit, abr[:, b:b + 1], 1.0)
            f_im = jnp.where(bit, abi[:, b:b + 1], 0.0)
            if pw_re is None:
                pw_re, pw_im = f_re, f_im
            else:
                pw_re, pw_im = pw_re * f_re - pw_im * f_im, pw_re * f_im + pw_im * f_re
        br = jnp.concatenate([bbr_ref[g]] * lane_reps, axis=1)
        bi = jnp.concatenate([bbi_ref[g]] * lane_reps, axis=1)
        return jnp.concatenate([pw_re * br - pw_im * bi, pw_re * bi + pw_im * br], axis=0)

    def output_weights(g):
        crr = crr_ref[g]
        cii = cii_ref[g]
        p1 = p1_ref[g]
        p2 = p2_ref[g]
        return jnp.concatenate(
            [(crr * p1[i:i + 1, :] + cii * p2[i:i + 1, :]).astype(BF16) for i in range(t)], axis=0)

    def local_products(g, slot):
        mt = mt_scr.at[slot]
        ws = state_weights(g)
        strip = jnp.dot(ccat_ref[g], ws, precision=HIGHEST, preferred_element_type=F32)
        tail = strip[:, S5_TW - LANES:] + jnp.where(lag0, skip_ref[g], 0.0)
        full = jnp.concatenate([strip[:, :S5_TW - LANES], tail, zero_half], axis=1)
        rot = [full] + [pltpu.roll(full, 2 * S5_TW - S5_GROUP * r, 1) for r in range(1, per_tile)]
        for i in range(t):
            a, r = divmod(t - 1 - i, per_tile)
            mt[i * S5_GROUP:(i + 1) * S5_GROUP, :] = (
                rot[r][:, a * LANES:a * LANES + S5_TW].astype(BF16))
        rhs = rhs_scr[g]
        yt_scr[g] = _dot(mt[...], rhs)
        return _dot(ws.astype(BF16), rhs)

    def scan_pair(gp, g0, s0, s1):
        xr = jnp.concatenate([s0[:S5_STATE], s1[:S5_STATE]], axis=0).T
        xi = jnp.concatenate([s0[S5_STATE:], s1[S5_STATE:]], axis=0).T
        apr = apr_ref[gp]
        api = api_ref[gp]
        for k in range(n_steps):
            ar = apr[k:k + 1, :]
            ai = api[k:k + 1, :]
            sr = shift_down(xr, 1 << k)
            si = shift_down(xi, 1 << k)
            xr, xi = xr + (ar * sr - ai * si), xi + (ar * si + ai * sr)
        pr = shift_down(xr, 1).T
        pi = shift_down(xi, 1).T
        for j in range(2):
            rows = slice(j * S5_STATE, (j + 1) * S5_STATE)
            xprev = jnp.concatenate([pr[rows], pi[rows]], axis=0).astype(BF16)
            yt_scr[g0 + j] += _dot(output_weights(g0 + j), xprev)

    def per_batch(b, carry):
        g0 = S5_BATCH * b
        s = [local_products(g0 + j, j) for j in range(S5_BATCH)]
        for p in range(S5_BATCH // 2):
            scan_pair((S5_BATCH // 2) * b + p, g0 + 2 * p, s[2 * p], s[2 * p + 1])
        return carry

    lax.fori_loop(0, S5_GB // S5_BATCH, per_batch, 0)

    for blk in range(n_blocks):
        cols = slice(blk * nb, (blk + 1) * nb)
        per_q = []
        for q in range(nq):
            w = []
            for s in range(SUBLANES):
                i = q * SUBLANES + s
                piece = yt_scr[:, i * S5_GROUP:(i + 1) * S5_GROUP, cols].reshape(S5_GB * S5_GROUP, nb)
                w.append(piece.T.reshape(nsub, SUBLANES, LANES))
            per_q.append(_sublane_transpose(w))
        x = jnp.stack([jnp.stack([per_q[q][r] for q in range(nq)], axis=1)
                       for r in range(SUBLANES)], axis=1)
        o_ref[blk * rb:(blk + 1) * rb, :] = x.reshape(rb, LANES).astype(o_ref.dtype)


def _s5core(proj, tables):
    seq = proj.shape[0]
    per_group, ap_re, ap_im, n_steps = tables
    nch = seq // S5_T
    lanes = S5_GB * S5_GROUP
    grp = lambda i: (i, 0, 0)
    gspec = lambda a: pl.BlockSpec((S5_GB,) + a.shape[1:], grp)
    pspec = lambda a: pl.BlockSpec((S5_GB // 2,) + a.shape[1:], grp)
    return pl.pallas_call(
        functools.partial(_s5core_kernel, n_steps),
        grid=(S5_GROUPS // S5_GB,),
        in_specs=([pl.BlockSpec((seq, lanes), lambda i: (0, i))] + [gspec(a) for a in per_group]
                  + [pspec(ap_re), pspec(ap_im)]),
        out_specs=pl.BlockSpec((seq, lanes), lambda i: (0, i)),
        out_shape=jax.ShapeDtypeStruct((seq, S5_WIDTH), BF16),
        scratch_shapes=[
            pltpu.VMEM((S5_GB, S5_TW, nch), BF16),
            pltpu.VMEM((S5_BATCH, S5_TW, S5_TW), BF16),
            pltpu.VMEM((S5_GB, S5_TW, nch), F32),
        ],
        compiler_params=pltpu.CompilerParams(
            dimension_semantics=("parallel",), vmem_limit_bytes=VMEM_LIMIT),
        name="s5_core",
    )(proj, *per_group, ap_re, ap_im)


def _gelu_tanh(y):
    return 0.5 * y * (1.0 + jnp.tanh(math.sqrt(2.0 / math.pi) * (y + 0.044715 * (y * y * y))))


def _s5post_block(y_ref, z_ref, wglu_ref, bglu_ref, o_ref):
    g = _gelu_tanh(y_ref[...].astype(F32))
    gate = jax.nn.sigmoid(_dot(g.astype(BF16), wglu_ref[...]) + bglu_ref[...])
    o_ref[...] = (g * gate * jax.nn.silu(z_ref[...].astype(F32))).astype(o_ref.dtype)


def _ret_tables(seq):
    t = RET_T
    inv = ROPE_BASE ** (-jnp.arange(0, RET_QK_HEAD, 2, dtype=F32) / RET_QK_HEAD)
    tile4 = lambda a: jnp.concatenate([a, a, a, a], axis=-1)
    ang_r = jnp.arange(t, dtype=F32)[:, None] * inv[None, :]
    ang_b = (jnp.arange(seq // t, dtype=F32) * float(t))[:, None] * inv[None, :]
    cos_r, sin_r = tile4(jnp.cos(ang_r)), tile4(jnp.sin(ang_r))
    cos_b, sin_b = tile4(jnp.cos(ang_b))[:, None, :], tile4(jnp.sin(ang_b))[:, None, :]
    half = RET_QK_HEAD // 2
    sign = jnp.asarray(np.tile(np.repeat([-1.0, 1.0], half), 2)[None, :], F32)

    log_g = np.log1p(-np.exp2(-5.0 - np.arange(RET_HEADS, dtype=np.float64)))
    i = np.arange(t)
    ci = i // CHUNK
    diff = (i[:, None] - i[None, :]).astype(np.float64)
    same = ci[:, None] == ci[None, :]
    earlier = ci[None, :] < ci[:, None]
    expo = np.where(same, np.abs(diff), diff)
    dmask = np.where((same | earlier)[None], np.exp(expo[None] * log_g[:, None, None]), 0.0)
    xi = np.exp((i + 1.0)[None, :] * log_g[:, None])
    zeta = np.exp((t - 1.0 - i)[None, :] * log_g[:, None])
    xi = np.broadcast_to(xi[:, :, None], (RET_HEADS, t, RET_V_HEAD))
    zeta = np.broadcast_to(zeta[:, :, None], (RET_HEADS, t, RET_V_HEAD))
    block_decay = [float(v) for v in np.exp(t * log_g)]
    return (cos_r, sin_r, cos_b, sin_b, sign, jnp.asarray(dmask, F32), jnp.asarray(xi, F32),
            jnp.asarray(zeta, F32), block_decay)


def _swap_halves(x):
    n = x.shape[1]
    half = RET_QK_HEAD // 2
    fwd = pltpu.roll(x, half, 1)
    bwd = pltpu.roll(x, n - half, 1)
    lane = lax.broadcasted_iota(jnp.int32, x.shape, 1)
    return jnp.where((lane & (RET_QK_HEAD - 1)) < half, bwd, fwd)


def _retention_block(block_decay, q_ref, k_ref, v_ref, z_ref, cosr_ref, sinr_ref, cosb_ref, sinb_ref,
                     sign_ref, dm_ref, xi_ref, zeta_ref, o_ref, state_ref):
    cb = cosb_ref[...]
    sb = sinb_ref[...]
    cos2 = cb * cosr_ref[...] - sb * sinr_ref[...]
    sin2 = (sb * cosr_ref[...] + cb * sinr_ref[...]) * sign_ref[...]
    reps = RET_QK // cos2.shape[1]
    cos = jnp.concatenate([cos2] * reps, axis=1)
    sin = jnp.concatenate([sin2] * reps, axis=1)
    q = q_ref[...].astype(F32)
    k = k_ref[...].astype(F32)
    qr = (q * cos + _swap_halves(q) * sin).astype(BF16)
    kr = ((k * cos + _swap_halves(k) * sin) * (RET_QK_HEAD ** -0.5)).astype(BF16)
    for h in range(RET_HEADS):
        qs = slice(h * RET_QK_HEAD, (h + 1) * RET_QK_HEAD)
        vs = slice(h * RET_V_HEAD, (h + 1) * RET_V_HEAD)
        qh = qr[:, qs]
        kh = kr[:, qs]
        vh = v_ref[:, vs]
        s = lax.dot_general(qh, kh, (((1,), (1,)), ((), ())), preferred_element_type=F32)
        s = s * dm_ref[h]
        inner = _dot(s.astype(BF16), vh)
        st = state_ref[h]
        cross = _dot(qh, st.astype(BF16)) * xi_ref[h]
        vz = (vh.astype(F32) * zeta_ref[h]).astype(BF16)
        kv = lax.dot_general(kh, vz, (((0,), (0,)), ((), ())), preferred_element_type=F32)
        state_ref[h] = st * block_decay[h] + kv
        o = inner + cross
        o = o * lax.rsqrt(jnp.mean(o * o, axis=-1, keepdims=True) + EPS)
        o_ref[:, vs] = (o * jax.nn.silu(z_ref[:, vs].astype(F32))).astype(o_ref.dtype)


def _merge_block(ys_ref, yr_ref, gs0_ref, gs1_ref, gr0_ref, gr1_ref, x_ref, ws_ref, wr_ref, wo_ref,
                 g_ref, o_ref):
    a = _dot(ys_ref[...], ws_ref[...])
    b = _dot(yr_ref[...], wr_ref[...])
    gs = jnp.concatenate([gs0_ref[...], gs1_ref[...]], axis=1).astype(F32)
    gr = jnp.concatenate([gr0_ref[...], gr1_ref[...]], axis=1).astype(F32)
    m = jax.nn.sigmoid(gs) * a + jax.nn.sigmoid(gr) * b
    o = _dot(m.astype(BF16), wo_ref[...])
    o_ref[...] = x_ref[...] + _rms_rows(o, g_ref[...])


def _tail_kernel(block_decay,
                 yssm_ref, zs_ref, q_ref, k_ref, v_ref, zr_ref, cosb_ref, sinb_ref,
                 gs0_ref, gs1_ref, gr0_ref, gr1_ref, x_ref,
                 wglu_ref, bglu_ref, cosr_ref, sinr_ref, sign_ref, dm_ref, xi_ref, zeta_ref,
                 ws_ref, wr_ref, wo_ref, g_ref,
                 o_ref, state_ref, ys_scr, yr_scr):
    @pl.when(pl.program_id(0) == 0)
    def _():
        state_ref[...] = jnp.zeros_like(state_ref)
        ys_scr[...] = jnp.zeros_like(ys_scr)
        yr_scr[...] = jnp.zeros_like(yr_scr)

    _merge_block(ys_scr, yr_scr, gs0_ref, gs1_ref, gr0_ref, gr1_ref, x_ref, ws_ref, wr_ref, wo_ref,
                 g_ref, o_ref)
    _s5post_block(yssm_ref, zs_ref, wglu_ref, bglu_ref, ys_scr)
    _retention_block(block_decay, q_ref, k_ref, v_ref, zr_ref, cosr_ref, sinr_ref, cosb_ref, sinb_ref,
                     sign_ref, dm_ref, xi_ref, zeta_ref, yr_scr, state_ref)


def _tail(y_ssm, proj, x2, w_glu, b_glu, w_s, w_r, w_o, gain):
    seq, d = x2.shape
    t = RET_T
    nblk = seq // t
    cos_r, sin_r, cos_b, sin_b, sign, dmask, xi, zeta, block_decay = _ret_tables(seq)
    blk = 1024
    cur = lambda s: jnp.minimum(s, nblk - 1)
    prv = lambda s: jnp.maximum(s - 1, 0)
    single = pl.Buffered(1)
    const = lambda a: pl.BlockSpec(a.shape, lambda s: (0,) * a.ndim, pipeline_mode=single)
    b_glu2 = b_glu.reshape(1, S5_WIDTH).astype(F32)
    gain2 = gain.reshape(1, d).astype(F32)
    return pl.pallas_call(
        functools.partial(_tail_kernel, block_decay),
        grid=(nblk + 1,),
        in_specs=[
            pl.BlockSpec((t, S5_WIDTH), lambda s: (cur(s), 0)),
            pl.BlockSpec((t, S5_WIDTH), lambda s: (cur(s), _ZS_BLK)),
            pl.BlockSpec((t, RET_QK), lambda s: (cur(s), _Q_BLK512)),
            pl.BlockSpec((t, RET_QK), lambda s: (cur(s), _K_BLK512)),
            pl.BlockSpec((t, RET_WIDTH), lambda s: (cur(s), _V_BLK)),
            pl.BlockSpec((t, RET_WIDTH), lambda s: (cur(s), _ZR_BLK)),
            pl.BlockSpec((None, 1, cos_b.shape[2]), lambda s: (cur(s), 0, 0)),
            pl.BlockSpec((None, 1, sin_b.shape[2]), lambda s: (cur(s), 0, 0)),
            pl.BlockSpec((t, blk), lambda s: (prv(s), _GLS_BLK)),
            pl.BlockSpec((t, blk), lambda s: (prv(s), _GLS_BLK + 1)),
            pl.BlockSpec((t, blk), lambda s: (prv(s), _GLR_BLK)),
            pl.BlockSpec((t, blk), lambda s: (prv(s), _GLR_BLK + 1)),
            pl.BlockSpec((t, d), lambda s: (prv(s), 0)),
            const(w_glu), const(b_glu2), const(cos_r), const(sin_r), const(sign), const(dmask),
            const(xi), const(zeta), const(w_s), const(w_r), const(w_o), const(gain2),
        ],
        out_specs=pl.BlockSpec((t, d), lambda s: (prv(s), 0)),
        out_shape=jax.ShapeDtypeStruct((seq, d), F32),
        scratch_shapes=[
            pltpu.VMEM((RET_HEADS, RET_QK_HEAD, RET_V_HEAD), F32),
            pltpu.VMEM((t, S5_WIDTH), BF16),
            pltpu.VMEM((t, RET_WIDTH), BF16),
        ],
        compiler_params=pltpu.CompilerParams(
            dimension_semantics=("arbitrary",), vmem_limit_bytes=VMEM_LIMIT),
        name="tail",
    )(y_ssm, proj, proj, proj, proj, proj, cos_b, sin_b, proj, proj, proj, proj, x2,
      w_glu, b_glu2, cos_r, sin_r, sign, dmask, xi, zeta, w_s, w_r, w_o, gain2)


def kernel(x, norm_pre, w_in, lam_re, lam_im, log_step, s5_b_re, s5_b_im, s5_c_re, s5_c_im, s5_d,
           w_glu, b_glu, w_proj_s5, w_proj_ret, w_out, norm_post):
    bsz, seq, d = x.shape
    assert bsz == 1 and d == D_MODEL and seq % 1024 == 0
    depth = w_in.shape[0]
    x2 = x.reshape(seq, d)
    for l in range(depth):
        proj, (w_glu_b, w_s_b, w_r_b, w_o_b) = _inproj(
            x2, norm_pre[l], w_in[l].astype(BF16), (w_glu[l], w_proj_s5[l], w_proj_ret[l], w_out[l]))
        tables = _s5_tables(lam_re[l], lam_im[l], log_step[l], s5_b_re[l], s5_b_im[l],
                            s5_c_re[l], s5_c_im[l], s5_d[l], seq // S5_T)
        y_ssm = _s5core(proj, tables)
        x2 = _tail(y_ssm, proj, x2, w_glu_b, b_glu[l], w_s_b, w_r_b, w_o_b, norm_post[l])
    return x2.reshape(bsz, seq, d)
```

```python
import functools
import math

import numpy as np
import jax
import jax.numpy as jnp
from jax import lax
from jax.experimental import pallas as pl
from jax.experimental.pallas import tpu as pltpu

F32 = jnp.float32
BF16 = jnp.bfloat16
HIGHEST = lax.Precision.HIGHEST

D_MODEL = 2048
EPS = 1e-6
CHUNK = 64

S5_WIDTH = 1024
S5_GROUP = 16
S5_GROUPS = 64
S5_STATE = 64

RET_HEADS = 8
RET_V_HEAD = 128
RET_QK_HEAD = 64
RET_QK = 512
RET_WIDTH = 1024
ROPE_BASE = 10000.0

_Q_BLK512 = (2 * S5_WIDTH) // 512
_K_BLK512 = (2 * S5_WIDTH + RET_QK) // 512
_ZS_BLK = 1
_V_BLK = (2 * S5_WIDTH + 2 * RET_QK) // 1024
_ZR_BLK = _V_BLK + 1
_GLS_BLK = _ZR_BLK + 1
_GLR_BLK = _GLS_BLK + 2

VMEM_LIMIT = 56 * 1024 * 1024
INPROJ_VMEM_LIMIT = 60 * 1024 * 1024
INPROJ_PIECE = 1280
INPROJ_ROWS = 256
SUBLANES = 8
LANES = 128

S5_T = 32
S5_GB = 8
S5_TW = S5_T * S5_GROUP
S5_RB = 4096
S5_SCAN_ROWS = 16
S5_BATCH = 4

RET_T = 256


def _dot(a, b):
    return jnp.dot(a, b, preferred_element_type=F32)


def _rms_rows(x, gain):
    ms = jnp.mean(x * x, axis=-1, keepdims=True)
    return x * lax.rsqrt(ms + EPS) * gain


def _inproj_kernel(n_extra, x_ref, g_ref, w_ref, *rest):
    extra_in = rest[:n_extra]
    o_ref = rest[n_extra]
    extra_out = rest[n_extra + 1:2 * n_extra + 1]
    h_ref = rest[2 * n_extra + 1]

    tm, tn = o_ref.shape
    pieces = [(lo, min(lo + INPROJ_PIECE, tn)) for lo in range(0, tn, INPROJ_PIECE)]

    @pl.when(pl.program_id(1) == 0)
    def _():
        for r0 in range(0, tm, INPROJ_ROWS):
            rows = slice(r0, r0 + INPROJ_ROWS)
            hc = _rms_rows(x_ref[rows, :], g_ref[...]).astype(BF16)
            h_ref[rows, :] = hc
            for lo, hi in pieces:
                o_ref[rows, lo:hi] = _dot(hc, w_ref[:, lo:hi]).astype(o_ref.dtype)

    @pl.when(pl.program_id(1) != 0)
    def _():
        for lo, hi in pieces:
            o_ref[:, lo:hi] = _dot(h_ref[...], w_ref[:, lo:hi]).astype(o_ref.dtype)

    for src, dst in zip(extra_in, extra_out):
        dst[...] = src[...].astype(dst.dtype)


def _inproj(x2, gain, w_bf16, later_weights, tm=1024, tn=2304):
    seq, d = x2.shape
    n = w_bf16.shape[1]
    grid = (seq // tm, n // tn)
    steps = grid[0] * grid[1]
    step_block = lambda i, j: (i * grid[1] + j, 0)
    extra_specs = [pl.BlockSpec((w.shape[0] // steps, w.shape[1]), step_block) for w in later_weights]
    outs = pl.pallas_call(
        functools.partial(_inproj_kernel, len(later_weights)),
        grid=grid,
        in_specs=[
            pl.BlockSpec((tm, d), lambda i, j: (i, 0)),
            pl.BlockSpec((1, d), lambda i, j: (0, 0)),
            pl.BlockSpec((d, tn), lambda i, j: (0, j)),
        ] + extra_specs,
        out_specs=[pl.BlockSpec((tm, tn), lambda i, j: (i, j))] + extra_specs,
        out_shape=[jax.ShapeDtypeStruct((seq, n), BF16)]
        + [jax.ShapeDtypeStruct(w.shape, BF16) for w in later_weights],
        scratch_shapes=[pltpu.VMEM((tm, d), BF16)],
        compiler_params=pltpu.CompilerParams(
            dimension_semantics=("arbitrary", "arbitrary"), vmem_limit_bytes=INPROJ_VMEM_LIMIT),
        name="inproj",
    )(x2, gain.reshape(1, d).astype(F32), w_bf16, *later_weights)
    return outs[0], outs[1:]


def _s5_tables(lam_re, lam_im, log_step, b_re, b_im, c_re, c_im, d_skip, n_chunks):
    t = S5_T
    lr = lam_re.astype(F32)
    li = lam_im.astype(F32)
    step = jnp.exp(log_step.astype(F32))[:, None]
    lrs = lr * step
    ang = li * step
    mag = jnp.exp(lrs)
    ab_re = mag * jnp.cos(ang)
    ab_im = mag * jnp.sin(ang)
    den = lr * lr + li * li
    nr = ab_re - 1.0
    f_re = (nr * lr + ab_im * li) / den
    f_im = (ab_im * lr - nr * li) / den
    br = b_re.astype(F32)
    bi = b_im.astype(F32)
    bb_re = f_re[..., None] * br - f_im[..., None] * bi
    bb_im = f_re[..., None] * bi + f_im[..., None] * br
    cr = c_re.astype(F32)
    ci = c_im.astype(F32)

    n_bits = int(math.log2(t))
    bit_re, bit_im = [ab_re], [ab_im]
    for _ in range(n_bits - 1):
        r, i = bit_re[-1], bit_im[-1]
        bit_re.append(r * r - i * i)
        bit_im.append(2.0 * r * i)
    pad_bits = ((0, 0), (0, 0), (0, SUBLANES - n_bits))
    abit_re = jnp.pad(jnp.stack(bit_re, axis=-1), pad_bits)
    abit_im = jnp.pad(jnp.stack(bit_im, axis=-1), pad_bits)
    bbt_re = jnp.tile(bb_re, (1, 1, LANES // S5_GROUP))
    bbt_im = jnp.tile(bb_im, (1, 1, LANES // S5_GROUP))
    e_ip = (jnp.arange(t, dtype=F32) + 1.0)[None, :, None]
    m_ip = jnp.exp(e_ip * lrs[:, None, :])
    pr = m_ip * jnp.cos(e_ip * ang[:, None, :])
    pi = m_ip * jnp.sin(e_ip * ang[:, None, :])
    p1 = jnp.concatenate([pr, -pi], axis=2)
    p2 = jnp.concatenate([-pi, -pr], axis=2)
    c_rr = jnp.concatenate([cr, cr], axis=2)
    c_ii = jnp.concatenate([ci, ci], axis=2)

    c_cat = jnp.concatenate([cr, -ci], axis=2)
    skip = jnp.broadcast_to(d_skip.astype(F32).reshape(S5_GROUPS, S5_GROUP, 1),
                            (S5_GROUPS, S5_GROUP, LANES))

    n_steps = max(1, int(math.ceil(math.log2(n_chunks))))
    m_t = jnp.exp(float(t) * lrs)
    sq_re, sq_im = [m_t * jnp.cos(float(t) * ang)], [m_t * jnp.sin(float(t) * ang)]
    for _ in range(n_steps - 1):
        r, i = sq_re[-1], sq_im[-1]
        sq_re.append(r * r - i * i)
        sq_im.append(2.0 * r * i)
    def pair_rows(sq):
        a = jnp.stack(sq, axis=0).reshape(n_steps, S5_GROUPS // 2, 2 * S5_STATE)
        a = jnp.transpose(a, (1, 0, 2))
        return jnp.pad(a, ((0, 0), (0, S5_SCAN_ROWS - n_steps), (0, 0)))

    per_group = (abit_re, abit_im, bbt_re, bbt_im, p1, p2, c_rr, c_ii, c_cat, skip)
    return per_group, pair_rows(sq_re), pair_rows(sq_im), n_steps


def _sublane_transpose(v):
    v = list(v)
    sub = lax.broadcasted_iota(jnp.int32, v[0].shape, 1)
    for k in (4, 2, 1):
        keep = (sub & k) == 0
        for r in range(SUBLANES):
            if r & k:
                continue
            a, b = v[r], v[r | k]
            v[r] = jnp.where(keep, a, pltpu.roll(b, k, 1))
            v[r | k] = jnp.where(keep, pltpu.roll(a, SUBLANES - k, 1), b)
    return v


def _s5core_kernel(n_steps, u_ref, abr_ref, abi_ref, bbr_ref, bbi_ref, p1_ref, p2_ref, crr_ref, cii_ref,
                   ccat_ref, skip_ref, apr_ref, api_ref, o_ref, rhs_scr, mt_scr, yt_scr):
    t = S5_T
    nch = rhs_scr.shape[2]
    nq = t // SUBLANES
    rb = min(S5_RB, nch * t)
    nb = rb // t
    nsub = nb // SUBLANES
    n_blocks = (nch * t) // rb

    for blk in range(n_blocks):
        x = u_ref[blk * rb:(blk + 1) * rb, :].astype(F32).reshape(nsub, SUBLANES, nq, SUBLANES, LANES)
        cols = slice(blk * nb, (blk + 1) * nb)
        for q in range(nq):
            w = _sublane_transpose([x[:, r, q] for r in range(SUBLANES)])
            for s in range(SUBLANES):
                i = q * SUBLANES + s
                rhs_scr[:, i * S5_GROUP:(i + 1) * S5_GROUP, cols] = (
                    w[s].reshape(nb, LANES).T.astype(BF16).reshape(S5_GB, S5_GROUP, nb))

    tail_lane = lax.broadcasted_iota(jnp.int32, (S5_GROUP, LANES), 1)
    tail_row = lax.broadcasted_iota(jnp.int32, (S5_GROUP, LANES), 0)
    lag0 = tail_lane == tail_row + (LANES - S5_GROUP)
    zero_half = jnp.zeros((S5_GROUP, S5_TW), F32)
    per_tile = LANES // S5_GROUP
    chunk_row = lax.broadcasted_iota(jnp.int32, (nch, LANES), 0)

    def shift_down(v, k):
        if k % SUBLANES == 0:
            return jnp.concatenate([jnp.zeros((k, LANES), F32), v[:nch - k]], axis=0)
        return jnp.where(chunk_row >= k, pltpu.roll(v, k, 0), 0.0)

    lane_expo = (t - 1) - lax.shift_right_logical(
        lax.broadcasted_iota(jnp.int32, (S5_STATE, S5_TW), 1), int(math.log2(S5_GROUP)))
    lane_reps = S5_TW // LANES

    def state_weights(g):
        abr = abr_ref[g]
        abi = abi_ref[g]
        pw_re = pw_im = None
        for b in range(int(math.log2(t))):
            bit = (lax.shift_right_logical(lane_expo, b) & 1) == 1
            f_re = jnp.where(bit, abr[:, b:b + 1], 1.0)
            f_im = jnp.where(bit, abi[:, b:b + 1], 0.0)
            if pw_re is None:
                pw_re, pw_im = f_re, f_im
            else:
                pw_re, pw_im = pw_re * f_re - pw_im * f_im, pw_re * f_im + pw_im * f_re
        br = jnp.concatenate([bbr_ref[g]] * lane_reps, axis=1)
        bi = jnp.concatenate([bbi_ref[g]] * lane_reps, axis=1)
        return jnp.concatenate([pw_re * br - pw_im * bi, pw_re * bi + pw_im * br], axis=0)

    def output_weights(g):
        crr = crr_ref[g]
        cii = cii_ref[g]
        p1 = p1_ref[g]
        p2 = p2_ref[g]
        return jnp.concatenate(
            [(crr * p1[i:i + 1, :] + cii * p2[i:i + 1, :]).astype(BF16) for i in range(t)], axis=0)

    def local_products(g, slot):
        mt = mt_scr.at[slot]
        ws = state_weights(g)
        strip = jnp.dot(ccat_ref[g], ws, precision=HIGHEST, preferred_element_type=F32)
        tail = strip[:, S5_TW - LANES:] + jnp.where(lag0, skip_ref[g], 0.0)
        full = jnp.concatenate([strip[:, :S5_TW - LANES], tail, zero_half], axis=1)
        rot = [full] + [pltpu.roll(full, 2 * S5_TW - S5_GROUP * r, 1) for r in range(1, per_tile)]
        for i in range(t):
            a, r = divmod(t - 1 - i, per_tile)
            mt[i * S5_GROUP:(i + 1) * S5_GROUP, :] = (
                rot[r][:, a * LANES:a * LANES + S5_TW].astype(BF16))
        rhs = rhs_scr[g]
        yt_scr[g] = _dot(mt[...], rhs)
        return _dot(ws.astype(BF16), rhs)

    def scan_pair(gp, g0, s0, s1):
        xr = jnp.concatenate([s0[:S5_STATE], s1[:S5_STATE]], axis=0).T
        xi = jnp.concatenate([s0[S5_STATE:], s1[S5_STATE:]], axis=0).T
        apr = apr_ref[gp]
        api = api_ref[gp]
        for k in range(n_steps):
            ar = apr[k:k + 1, :]
            ai = api[k:k + 1, :]
            sr = shift_down(xr, 1 << k)
            si = shift_down(xi, 1 << k)
            xr, xi = xr + (ar * sr - ai * si), xi + (ar * si + ai * sr)
        pr = shift_down(xr, 1).T
        pi = shift_down(xi, 1).T
        for j in range(2):
            rows = slice(j * S5_STATE, (j + 1) * S5_STATE)
            xprev = jnp.concatenate([pr[rows], pi[rows]], axis=0).astype(BF16)
            yt_scr[g0 + j] += _dot(output_weights(g0 + j), xprev)

    def per_batch(b, carry):
        g0 = S5_BATCH * b
        s = [local_products(g0 + j, j) for j in range(S5_BATCH)]
        for p in range(S5_BATCH // 2):
            scan_pair((S5_BATCH // 2) * b + p, g0 + 2 * p, s[2 * p], s[2 * p + 1])
        return carry

    lax.fori_loop(0, S5_GB // S5_BATCH, per_batch, 0)

    for blk in range(n_blocks):
        cols = slice(blk * nb, (blk + 1) * nb)
        per_q = []
        for q in range(nq):
            w = []
            for s in range(SUBLANES):
                i = q * SUBLANES + s
                piece = yt_scr[:, i * S5_GROUP:(i + 1) * S5_GROUP, cols].reshape(S5_GB * S5_GROUP, nb)
                w.append(piece.T.reshape(nsub, SUBLANES, LANES))
            per_q.append(_sublane_transpose(w))
        x = jnp.stack([jnp.stack([per_q[q][r] for q in range(nq)], axis=1)
                       for r in range(SUBLANES)], axis=1)
        o_ref[blk * rb:(blk + 1) * rb, :] = x.reshape(rb, LANES).astype(o_ref.dtype)


def _s5core(proj, tables):
    seq = proj.shape[0]
    per_group, ap_re, ap_im, n_steps = tables
    nch = seq // S5_T
    lanes = S5_GB * S5_GROUP
    grp = lambda i: (i, 0, 0)
    gspec = lambda a: pl.BlockSpec((S5_GB,) + a.shape[1:], grp)
    pspec = lambda a: pl.BlockSpec((S5_GB // 2,) + a.shape[1:], grp)
    return pl.pallas_call(
        functools.partial(_s5core_kernel, n_steps),
        grid=(S5_GROUPS // S5_GB,),
        in_specs=([pl.BlockSpec((seq, lanes), lambda i: (0, i))] + [gspec(a) for a in per_group]
                  + [pspec(ap_re), pspec(ap_im)]),
        out_specs=pl.BlockSpec((seq, lanes), lambda i: (0, i)),
        out_shape=jax.ShapeDtypeStruct((seq, S5_WIDTH), BF16),
        scratch_shapes=[
            pltpu.VMEM((S5_GB, S5_TW, nch), BF16),
            pltpu.VMEM((S5_BATCH, S5_TW, S5_TW), BF16),
            pltpu.VMEM((S5_GB, S5_TW, nch), F32),
        ],
        compiler_params=pltpu.CompilerParams(
            dimension_semantics=("parallel",), vmem_limit_bytes=VMEM_LIMIT),
        name="s5_core",
    )(proj, *per_group, ap_re, ap_im)


def _gelu_tanh(y):
    return 0.5 * y * (1.0 + jnp.tanh(math.sqrt(2.0 / math.pi) * (y + 0.044715 * (y * y * y))))


def _s5post_block(y_ref, z_ref, wglu_ref, bglu_ref, o_ref):
    g = _gelu_tanh(y_ref[...].astype(F32))
    gate = jax.nn.sigmoid(_dot(g.astype(BF16), wglu_ref[...]) + bglu_ref[...])
    o_ref[...] = (g * gate * jax.nn.silu(z_ref[...].astype(F32))).astype(o_ref.dtype)


def _ret_tables(seq):
    t = RET_T
    inv = ROPE_BASE ** (-jnp.arange(0, RET_QK_HEAD, 2, dtype=F32) / RET_QK_HEAD)
    tile4 = lambda a: jnp.concatenate([a, a, a, a], axis=-1)
    ang_r = jnp.arange(t, dtype=F32)[:, None] * inv[None, :]
    ang_b = (jnp.arange(seq // t, dtype=F32) * float(t))[:, None] * inv[None, :]
    cos_r, sin_r = tile4(jnp.cos(ang_r)), tile4(jnp.sin(ang_r))
    cos_b, sin_b = tile4(jnp.cos(ang_b))[:, None, :], tile4(jnp.sin(ang_b))[:, None, :]
    half = RET_QK_HEAD // 2
    sign = jnp.asarray(np.tile(np.repeat([-1.0, 1.0], half), 2)[None, :], F32)

    log_g = np.log1p(-np.exp2(-5.0 - np.arange(RET_HEADS, dtype=np.float64)))
    i = np.arange(t)
    ci = i // CHUNK
    diff = (i[:, None] - i[None, :]).astype(np.float64)
    same = ci[:, None] == ci[None, :]
    earlier = ci[None, :] < ci[:, None]
    expo = np.where(same, np.abs(diff), diff)
    dmask = np.where((same | earlier)[None], np.exp(expo[None] * log_g[:, None, None]), 0.0)
    xi = np.exp((i + 1.0)[None, :] * log_g[:, None])
    zeta = np.exp((t - 1.0 - i)[None, :] * log_g[:, None])
    xi = np.broadcast_to(xi[:, :, None], (RET_HEADS, t, RET_V_HEAD))
    zeta = np.broadcast_to(zeta[:, :, None], (RET_HEADS, t, RET_V_HEAD))
    block_decay = [float(v) for v in np.exp(t * log_g)]
    return (cos_r, sin_r, cos_b, sin_b, sign, jnp.asarray(dmask, F32), jnp.asarray(xi, F32),
            jnp.asarray(zeta, F32), block_decay)


def _swap_halves(x):
    n = x.shape[1]
    half = RET_QK_HEAD // 2
    fwd = pltpu.roll(x, half, 1)
    bwd = pltpu.roll(x, n - half, 1)
    lane = lax.broadcasted_iota(jnp.int32, x.shape, 1)
    return jnp.where((lane & (RET_QK_HEAD - 1)) < half, bwd, fwd)


def _retention_block(block_decay, q_ref, k_ref, v_ref, z_ref, cosr_ref, sinr_ref, cosb_ref, sinb_ref,
                     sign_ref, dm_ref, xi_ref, zeta_ref, o_ref, state_ref):
    cb = cosb_ref[...]
    sb = sinb_ref[...]
    cos2 = cb * cosr_ref[...] - sb * sinr_ref[...]
    sin2 = (sb * cosr_ref[...] + cb * sinr_ref[...]) * sign_ref[...]
    reps = RET_QK // cos2.shape[1]
    cos = jnp.concatenate([cos2] * reps, axis=1)
    sin = jnp.concatenate([sin2] * reps, axis=1)
    q = q_ref[...].astype(F32)
    k = k_ref[...].astype(F32)
    qr = (q * cos + _swap_halves(q) * sin).astype(BF16)
    kr = ((k * cos + _swap_halves(k) * sin) * (RET_QK_HEAD ** -0.5)).astype(BF16)
    for h in range(RET_HEADS):
        qs = slice(h * RET_QK_HEAD, (h + 1) * RET_QK_HEAD)
        vs = slice(h * RET_V_HEAD, (h + 1) * RET_V_HEAD)
        qh = qr[:, qs]
        kh = kr[:, qs]
        vh = v_ref[:, vs]
        s = lax.dot_general(qh, kh, (((1,), (1,)), ((), ())), preferred_element_type=F32)
        s = s * dm_ref[h]
        inner = _dot(s.astype(BF16), vh)
        st = state_ref[h]
        cross = _dot(qh, st.astype(BF16)) * xi_ref[h]
        vz = (vh.astype(F32) * zeta_ref[h]).astype(BF16)
        kv = lax.dot_general(kh, vz, (((0,), (0,)), ((), ())), preferred_element_type=F32)
        state_ref[h] = st * block_decay[h] + kv
        o = inner + cross
        o = o * lax.rsqrt(jnp.mean(o * o, axis=-1, keepdims=True) + EPS)
        o_ref[:, vs] = (o * jax.nn.silu(z_ref[:, vs].astype(F32))).astype(o_ref.dtype)


def _merge_block(ys_ref, yr_ref, gs0_ref, gs1_ref, gr0_ref, gr1_ref, x_ref, ws_ref, wr_ref, wo_ref,
                 g_ref, o_ref):
    a = _dot(ys_ref[...], ws_ref[...])
    b = _dot(yr_ref[...], wr_ref[...])
    gs = jnp.concatenate([gs0_ref[...], gs1_ref[...]], axis=1).astype(F32)
    gr = jnp.concatenate([gr0_ref[...], gr1_ref[...]], axis=1).astype(F32)
    m = jax.nn.sigmoid(gs) * a + jax.nn.sigmoid(gr) * b
    o = _dot(m.astype(BF16), wo_ref[...])
    o_ref[...] = x_ref[...] + _rms_rows(o, g_ref[...])


def _tail_kernel(block_decay,
                 yssm_ref, zs_ref, q_ref, k_ref, v_ref, zr_ref, cosb_ref, sinb_ref,
                 gs0_ref, gs1_ref, gr0_ref, gr1_ref, x_ref,
                 wglu_ref, bglu_ref, cosr_ref, sinr_ref, sign_ref, dm_ref, xi_ref, zeta_ref,
                 ws_ref, wr_ref, wo_ref, g_ref,
                 o_ref, state_ref, ys_scr, yr_scr):
    @pl.when(pl.program_id(0) == 0)
    def _():
        state_ref[...] = jnp.zeros_like(state_ref)
        ys_scr[...] = jnp.zeros_like(ys_scr)
        yr_scr[...] = jnp.zeros_like(yr_scr)

    _merge_block(ys_scr, yr_scr, gs0_ref, gs1_ref, gr0_ref, gr1_ref, x_ref, ws_ref, wr_ref, wo_ref,
                 g_ref, o_ref)
    _s5post_block(yssm_ref, zs_ref, wglu_ref, bglu_ref, ys_scr)
    _retention_block(block_decay, q_ref, k_ref, v_ref, zr_ref, cosr_ref, sinr_ref, cosb_ref, sinb_ref,
                     sign_ref, dm_ref, xi_ref, zeta_ref, yr_scr, state_ref)


def _tail(y_ssm, proj, x2, w_glu, b_glu, w_s, w_r, w_o, gain):
    seq, d = x2.shape
    t = RET_T
    nblk = seq // t
    cos_r, sin_r, cos_b, sin_b, sign, dmask, xi, zeta, block_decay = _ret_tables(seq)
    blk = 1024
    cur = lambda s: jnp.minimum(s, nblk - 1)
    prv = lambda s: jnp.maximum(s - 1, 0)
    single = pl.Buffered(1)
    const = lambda a: pl.BlockSpec(a.shape, lambda s: (0,) * a.ndim, pipeline_mode=single)
    b_glu2 = b_glu.reshape(1, S5_WIDTH).astype(F32)
    gain2 = gain.reshape(1, d).astype(F32)
    return pl.pallas_call(
        functools.partial(_tail_kernel, block_decay),
        grid=(nblk + 1,),
        in_specs=[
            pl.BlockSpec((t, S5_WIDTH), lambda s: (cur(s), 0)),
            pl.BlockSpec((t, S5_WIDTH), lambda s: (cur(s), _ZS_BLK)),
            pl.BlockSpec((t, RET_QK), lambda s: (cur(s), _Q_BLK512)),
            pl.BlockSpec((t, RET_QK), lambda s: (cur(s), _K_BLK512)),
            pl.BlockSpec((t, RET_WIDTH), lambda s: (cur(s), _V_BLK)),
            pl.BlockSpec((t, RET_WIDTH), lambda s: (cur(s), _ZR_BLK)),
            pl.BlockSpec((None, 1, cos_b.shape[2]), lambda s: (cur(s), 0, 0)),
            pl.BlockSpec((None, 1, sin_b.shape[2]), lambda s: (cur(s), 0, 0)),
            pl.BlockSpec((t, blk), lambda s: (prv(s), _GLS_BLK)),
            pl.BlockSpec((t, blk), lambda s: (prv(s), _GLS_BLK + 1)),
            pl.BlockSpec((t, blk), lambda s: (prv(s), _GLR_BLK)),
            pl.BlockSpec((t, blk), lambda s: (prv(s), _GLR_BLK + 1)),
            pl.BlockSpec((t, d), lambda s: (prv(s), 0)),
            const(w_glu), const(b_glu2), const(cos_r), const(sin_r), const(sign), const(dmask),
            const(xi), const(zeta), const(w_s), const(w_r), const(w_o), const(gain2),
        ],
        out_specs=pl.BlockSpec((t, d), lambda s: (prv(s), 0)),
        out_shape=jax.ShapeDtypeStruct((seq, d), F32),
        scratch_shapes=[
            pltpu.VMEM((RET_HEADS, RET_QK_HEAD, RET_V_HEAD), F32),
            pltpu.VMEM((t, S5_WIDTH), BF16),
            pltpu.VMEM((t, RET_WIDTH), BF16),
        ],
        compiler_params=pltpu.CompilerParams(
            dimension_semantics=("arbitrary",), vmem_limit_bytes=VMEM_LIMIT),
        name="tail",
    )(y_ssm, proj, proj, proj, proj, proj, cos_b, sin_b, proj, proj, proj, proj, x2,
      w_glu, b_glu2, cos_r, sin_r, sign, dmask, xi, zeta, w_s, w_r, w_o, gain2)


def kernel(x, norm_pre, w_in, lam_re, lam_im, log_step, s5_b_re, s5_b_im, s5_c_re, s5_c_im, s5_d,
           w_glu, b_glu, w_proj_s5, w_proj_ret, w_out, norm_post):
    bsz, seq, d = x.shape
    assert bsz == 1 and d == D_MODEL and seq % 1024 == 0
    depth = w_in.shape[0]
    x2 = x.reshape(seq, d)
    for l in range(depth):
        proj, (w_glu_b, w_s_b, w_r_b, w_o_b) = _inproj(
            x2, norm_pre[l], w_in[l].astype(BF16), (w_glu[l], w_proj_s5[l], w_proj_ret[l], w_out[l]))
        tables = _s5_tables(lam_re[l], lam_im[l], log_step[l], s5_b_re[l], s5_b_im[l],
                            s5_c_re[l], s5_c_im[l], s5_d[l], seq // S5_T)
        y_ssm = _s5core(proj, tables)
        x2 = _tail(y_ssm, proj, x2, w_glu_b, b_glu[l], w_s_b, w_r_b, w_o_b, norm_post[l])
    return x2.reshape(bsz, seq, d)
```

```python
import functools
import math

import numpy as np
import jax
import jax.numpy as jnp
from jax import lax
from jax.experimental import pallas as pl
from jax.experimental.pallas import tpu as pltpu

F32 = jnp.float32
BF16 = jnp.bfloat16
HIGHEST = lax.Precision.HIGHEST

D_MODEL = 2048
EPS = 1e-6
CHUNK = 64

S5_WIDTH = 1024
S5_GROUP = 16
S5_GROUPS = 64
S5_STATE = 64

RET_HEADS = 8
RET_V_HEAD = 128
RET_QK_HEAD = 64
RET_QK = 512
RET_WIDTH = 1024
ROPE_BASE = 10000.0

_IN_QK = 2 * S5_WIDTH
_IN_V = _IN_QK + 2 * RET_QK
_IN_GL = _IN_V + 2 * RET_WIDTH
_IN_END = _IN_GL + 2 * D_MODEL
_PROJ_ORDER = ((_IN_GL, _IN_END), (_IN_V, _IN_GL), (0, _IN_QK), (_IN_QK, _IN_V))
_GL_OFF = 0
_VZ_OFF = _GL_OFF + 2 * D_MODEL
_US_OFF = _VZ_OFF + 2 * RET_WIDTH
_ZS_OFF = _US_OFF + S5_WIDTH
_QK_OFF = _ZS_OFF + S5_WIDTH

VMEM_LIMIT = 56 * 1024 * 1024
INPROJ_VMEM_LIMIT = 60 * 1024 * 1024
INPROJ_PIECE = 1280
INPROJ_ROWS = 256
SUBLANES = 8
LANES = 128

S5_T = 32
S5_GB = 8
S5_TW = S5_T * S5_GROUP
S5_RB = 4096
S5_SCAN_ROWS = 16
S5_BATCH = 4

RET_T = 256


def _dot(a, b):
    return jnp.dot(a, b, preferred_element_type=F32)


def _rms_rows(x, gain):
    ms = jnp.mean(x * x, axis=-1, keepdims=True)
    return x * lax.rsqrt(ms + EPS) * gain


def _inproj_kernel(n_extra, x_ref, g_ref, w_ref, *rest):
    extra_in = rest[:n_extra]
    o_ref = rest[n_extra]
    extra_out = rest[n_extra + 1:2 * n_extra + 1]
    h_ref = rest[2 * n_extra + 1]

    tm, tn = o_ref.shape
    pieces = [(lo, min(lo + INPROJ_PIECE, tn)) for lo in range(0, tn, INPROJ_PIECE)]

    @pl.when(pl.program_id(1) == 0)
    def _():
        for r0 in range(0, tm, INPROJ_ROWS):
            rows = slice(r0, r0 + INPROJ_ROWS)
            hc = _rms_rows(x_ref[rows, :], g_ref[...]).astype(BF16)
            h_ref[rows, :] = hc
            for lo, hi in pieces:
                o_ref[rows, lo:hi] = _dot(hc, w_ref[:, lo:hi]).astype(o_ref.dtype)

    @pl.when(pl.program_id(1) != 0)
    def _():
        for lo, hi in pieces:
            o_ref[:, lo:hi] = _dot(h_ref[...], w_ref[:, lo:hi]).astype(o_ref.dtype)

    for src, dst in zip(extra_in, extra_out):
        dst[...] = src[...].astype(dst.dtype)


def _inproj(x2, gain, w_bf16, later_weights, tm=1024, tn=2304):
    seq, d = x2.shape
    n = w_bf16.shape[1]
    grid = (seq // tm, n // tn)
    steps = grid[0] * grid[1]
    step_block = lambda i, j: (i * grid[1] + j, 0)
    extra_specs = [pl.BlockSpec((w.shape[0] // steps, w.shape[1]), step_block) for w in later_weights]
    outs = pl.pallas_call(
        functools.partial(_inproj_kernel, len(later_weights)),
        grid=grid,
        in_specs=[
            pl.BlockSpec((tm, d), lambda i, j: (i, 0)),
            pl.BlockSpec((1, d), lambda i, j: (0, 0)),
            pl.BlockSpec((d, tn), lambda i, j: (0, j)),
        ] + extra_specs,
        out_specs=[pl.BlockSpec((tm, tn), lambda i, j: (i, j))] + extra_specs,
        out_shape=[jax.ShapeDtypeStruct((seq, n), BF16)]
        + [jax.ShapeDtypeStruct(w.shape, BF16) for w in later_weights],
        scratch_shapes=[pltpu.VMEM((tm, d), BF16)],
        compiler_params=pltpu.CompilerParams(
            dimension_semantics=("arbitrary", "arbitrary"), vmem_limit_bytes=INPROJ_VMEM_LIMIT),
        name="inproj",
    )(x2, gain.reshape(1, d).astype(F32), w_bf16, *later_weights)
    return outs[0], outs[1:]


def _s5_tables(lam_re, lam_im, log_step, b_re, b_im, c_re, c_im, d_skip, n_chunks):
    t = S5_T
    lr = lam_re.astype(F32)
    li = lam_im.astype(F32)
    step = jnp.exp(log_step.astype(F32))[:, None]
    lrs = lr * step
    ang = li * step
    mag = jnp.exp(lrs)
    ab_re = mag * jnp.cos(ang)
    ab_im = mag * jnp.sin(ang)
    den = lr * lr + li * li
    nr = ab_re - 1.0
    f_re = (nr * lr + ab_im * li) / den
    f_im = (ab_im * lr - nr * li) / den
    br = b_re.astype(F32)
    bi = b_im.astype(F32)
    bb_re = f_re[..., None] * br - f_im[..., None] * bi
    bb_im = f_re[..., None] * bi + f_im[..., None] * br
    cr = c_re.astype(F32)
    ci = c_im.astype(F32)

    n_bits = int(math.log2(t))
    bit_re, bit_im = [ab_re], [ab_im]
    for _ in range(n_bits - 1):
        r, i = bit_re[-1], bit_im[-1]
        bit_re.append(r * r - i * i)
        bit_im.append(2.0 * r * i)
    pad_bits = ((0, 0), (0, 0), (0, SUBLANES - n_bits))
    abit_re = jnp.pad(jnp.stack(bit_re, axis=-1), pad_bits)
    abit_im = jnp.pad(jnp.stack(bit_im, axis=-1), pad_bits)
    bbt_re = jnp.tile(bb_re, (1, 1, LANES // S5_GROUP))
    bbt_im = jnp.tile(bb_im, (1, 1, LANES // S5_GROUP))
    e_ip = (jnp.arange(t, dtype=F32) + 1.0)[None, :, None]
    m_ip = jnp.exp(e_ip * lrs[:, None, :])
    pr = m_ip * jnp.cos(e_ip * ang[:, None, :])
    pi = m_ip * jnp.sin(e_ip * ang[:, None, :])
    p1 = jnp.concatenate([pr, -pi], axis=2)
    p2 = jnp.concatenate([-pi, -pr], axis=2)
    c_rr = jnp.concatenate([cr, cr], axis=2)
    c_ii = jnp.concatenate([ci, ci], axis=2)

    c_cat = jnp.concatenate([cr, -ci], axis=2)
    skip = jnp.broadcast_to(d_skip.astype(F32).reshape(S5_GROUPS, S5_GROUP, 1),
                            (S5_GROUPS, S5_GROUP, LANES))

    n_steps = max(1, int(math.ceil(math.log2(n_chunks))))
    m_t = jnp.exp(float(t) * lrs)
    sq_re, sq_im = [m_t * jnp.cos(float(t) * ang)], [m_t * jnp.sin(float(t) * ang)]
    for _ in range(n_steps - 1):
        r, i = sq_re[-1], sq_im[-1]
        sq_re.append(r * r - i * i)
        sq_im.append(2.0 * r * i)
    def pair_rows(sq):
        a = jnp.stack(sq, axis=0).reshape(n_steps, S5_GROUPS // 2, 2 * S5_STATE)
        a = jnp.transpose(a, (1, 0, 2))
        return jnp.pad(a, ((0, 0), (0, S5_SCAN_ROWS - n_steps), (0, 0)))

    per_group = (abit_re, abit_im, bbt_re, bbt_im, p1, p2, c_rr, c_ii, c_cat, skip)
    return per_group, pair_rows(sq_re), pair_rows(sq_im), n_steps


def _sublane_transpose(v):
    v = list(v)
    sub = lax.broadcasted_iota(jnp.int32, v[0].shape, 1)
    for k in (4, 2, 1):
        keep = (sub & k) == 0
        for r in range(SUBLANES):
            if r & k:
                continue
            a, b = v[r], v[r | k]
            v[r] = jnp.where(keep, a, pltpu.roll(b, k, 1))
            v[r | k] = jnp.where(keep, pltpu.roll(a, SUBLANES - k, 1), b)
    return v


def _s5core_kernel(n_steps, u_ref, abr_ref, abi_ref, bbr_ref, bbi_ref, p1_ref, p2_ref, crr_ref, cii_ref,
                   ccat_ref, skip_ref, apr_ref, api_ref, o_ref, rhs_scr, mt_scr, yt_scr):
    t = S5_T
    nch = rhs_scr.shape[2]
    nq = t // SUBLANES
    rb = min(S5_RB, nch * t)
    nb = rb // t
    nsub = nb // SUBLANES
    n_blocks = (nch * t) // rb

    for blk in range(n_blocks):
        x = u_ref[blk * rb:(blk + 1) * rb, :].astype(F32).reshape(nsub, SUBLANES, nq, SUBLANES, LANES)
        cols = slice(blk * nb, (blk + 1) * nb)
        for q in range(nq):
            w = _sublane_transpose([x[:, r, q] for r in range(SUBLANES)])
            for s in range(SUBLANES):
                i = q * SUBLANES + s
                rhs_scr[:, i * S5_GROUP:(i + 1) * S5_GROUP, cols] = (
                    w[s].reshape(nb, LANES).T.astype(BF16).reshape(S5_GB, S5_GROUP, nb))

    tail_lane = lax.broadcasted_iota(jnp.int32, (S5_GROUP, LANES), 1)
    tail_row = lax.broadcasted_iota(jnp.int32, (S5_GROUP, LANES), 0)
    lag0 = tail_lane == tail_row + (LANES - S5_GROUP)
    zero_half = jnp.zeros((S5_GROUP, S5_TW), F32)
    per_tile = LANES // S5_GROUP
    chunk_row = lax.broadcasted_iota(jnp.int32, (nch, LANES), 0)

    def shift_down(v, k):
        if k % SUBLANES == 0:
            return jnp.concatenate([jnp.zeros((k, LANES), F32), v[:nch - k]], axis=0)
        return jnp.where(chunk_row >= k, pltpu.roll(v, k, 0), 0.0)

    lane_expo = (t - 1) - lax.shift_right_logical(
        lax.broadcasted_iota(jnp.int32, (S5_STATE, S5_TW), 1), int(math.log2(S5_GROUP)))
    lane_reps = S5_TW // LANES

    def state_weights(g):
        abr = abr_ref[g]
        abi = abi_ref[g]
        pw_re = pw_im = None
        for b in range(int(math.log2(t))):
            bit = (lax.shift_right_logical(lane_expo, b) & 1) == 1
            f_re = jnp.where(bit, abr[:, b:b + 1], 1.0)
            f_im = jnp.where(bit, abi[:, b:b + 1], 0.0)
            if pw_re is None:
                pw_re, pw_im = f_re, f_im
            else:
                pw_re, pw_im = pw_re * f_re - pw_im * f_im, pw_re * f_im + pw_im * f_re
        br = jnp.concatenate([bbr_ref[g]] * lane_reps, axis=1)
        bi = jnp.concatenate([bbi_ref[g]] * lane_reps, axis=1)
        return jnp.concatenate([pw_re * br - pw_im * bi, pw_re * bi + pw_im * br], axis=0)

    def output_weights(g):
        crr = crr_ref[g]
        cii = cii_ref[g]
        p1 = p1_ref[g]
        p2 = p2_ref[g]
        return jnp.concatenate(
            [(crr * p1[i:i + 1, :] + cii * p2[i:i + 1, :]).astype(BF16) for i in range(t)], axis=0)

    def local_products(g, slot):
        mt = mt_scr.at[slot]
        ws = state_weights(g)
        strip = jnp.dot(ccat_ref[g], ws, precision=HIGHEST, preferred_element_type=F32)
        tail = strip[:, S5_TW - LANES:] + jnp.where(lag0, skip_ref[g], 0.0)
        full = jnp.concatenate([strip[:, :S5_TW - LANES], tail, zero_half], axis=1)
        rot = [full] + [pltpu.roll(full, 2 * S5_TW - S5_GROUP * r, 1) for r in range(1, per_tile)]
        for i in range(t):
            a, r = divmod(t - 1 - i, per_tile)
            mt[i * S5_GROUP:(i + 1) * S5_GROUP, :] = (
                rot[r][:, a * LANES:a * LANES + S5_TW].astype(BF16))
        rhs = rhs_scr[g]
        yt_scr[g] = _dot(mt[...], rhs)
        return _dot(ws.astype(BF16), rhs)

    def scan_pair(gp, g0, s0, s1):
        xr = jnp.concatenate([s0[:S5_STATE], s1[:S5_STATE]], axis=0).T
        xi = jnp.concatenate([s0[S5_STATE:], s1[S5_STATE:]], axis=0).T
        apr = apr_ref[gp]
        api = api_ref[gp]
        for k in range(n_steps):
            ar = apr[k:k + 1, :]
            ai = api[k:k + 1, :]
            sr = shift_down(xr, 1 << k)
            si = shift_down(xi, 1 << k)
            xr, xi = xr + (ar * sr - ai * si), xi + (ar * si + ai * sr)
        pr = shift_down(xr, 1).T
        pi = shift_down(xi, 1).T
        for j in range(2):
            rows = slice(j * S5_STATE, (j + 1) * S5_STATE)
            xprev = jnp.concatenate([pr[rows], pi[rows]], axis=0).astype(BF16)
            yt_scr[g0 + j] += _dot(output_weights(g0 + j), xprev)

    def per_batch(b, carry):
        g0 = S5_BATCH * b
        s = [local_products(g0 + j, j) for j in range(S5_BATCH)]
        for p in range(S5_BATCH // 2):
            scan_pair((S5_BATCH // 2) * b + p, g0 + 2 * p, s[2 * p], s[2 * p + 1])
        return carry

    lax.fori_loop(0, S5_GB // S5_BATCH, per_batch, 0)

    for blk in range(n_blocks):
        cols = slice(blk * nb, (blk + 1) * nb)
        per_q = []
        for q in range(nq):
            w = []
            for s in range(SUBLANES):
                i = q * SUBLANES + s
                piece = yt_scr[:, i * S5_GROUP:(i + 1) * S5_GROUP, cols].reshape(S5_GB * S5_GROUP, nb)
                w.append(piece.T.reshape(nsub, SUBLANES, LANES))
            per_q.append(_sublane_transpose(w))
        x = jnp.stack([jnp.stack([per_q[q][r] for q in range(nq)], axis=1)
                       for r in range(SUBLANES)], axis=1)
        o_ref[blk * rb:(blk + 1) * rb, :] = x.reshape(rb, LANES).astype(o_ref.dtype)


def _s5core(proj, tables):
    seq = proj.shape[0]
    per_group, ap_re, ap_im, n_steps = tables
    nch = seq // S5_T
    lanes = S5_GB * S5_GROUP
    grp = lambda i: (i, 0, 0)
    gspec = lambda a: pl.BlockSpec((S5_GB,) + a.shape[1:], grp)
    pspec = lambda a: pl.BlockSpec((S5_GB // 2,) + a.shape[1:], grp)
    return pl.pallas_call(
        functools.partial(_s5core_kernel, n_steps),
        grid=(S5_GROUPS // S5_GB,),
        in_specs=([pl.BlockSpec((seq, lanes), lambda i: (0, _US_OFF // lanes + i))]
                  + [gspec(a) for a in per_group]
                  + [pspec(ap_re), pspec(ap_im)]),
        out_specs=pl.BlockSpec((seq, lanes), lambda i: (0, i)),
        out_shape=jax.ShapeDtypeStruct((seq, S5_WIDTH), BF16),
        scratch_shapes=[
            pltpu.VMEM((S5_GB, S5_TW, nch), BF16),
            pltpu.VMEM((S5_BATCH, S5_TW, S5_TW), BF16),
            pltpu.VMEM((S5_GB, S5_TW, nch), F32),
        ],
        compiler_params=pltpu.CompilerParams(
            dimension_semantics=("parallel",), vmem_limit_bytes=VMEM_LIMIT),
        name="s5_core",
    )(proj, *per_group, ap_re, ap_im)


def _gelu_tanh(y):
    return 0.5 * y * (1.0 + jnp.tanh(math.sqrt(2.0 / math.pi) * (y + 0.044715 * (y * y * y))))


def _s5post_block(y_ref, z_ref, wglu_ref, bglu_ref, o_ref):
    g = _gelu_tanh(y_ref[...].astype(F32))
    gate = jax.nn.sigmoid(_dot(g.astype(BF16), wglu_ref[...]) + bglu_ref[...])
    o_ref[...] = (g * gate * jax.nn.silu(z_ref[...].astype(F32))).astype(o_ref.dtype)


def _ret_tables(seq):
    t = RET_T
    inv = ROPE_BASE ** (-jnp.arange(0, RET_QK_HEAD, 2, dtype=F32) / RET_QK_HEAD)
    tile4 = lambda a: jnp.concatenate([a, a, a, a], axis=-1)
    ang_r = jnp.arange(t, dtype=F32)[:, None] * inv[None, :]
    ang_b = (jnp.arange(seq // t, dtype=F32) * float(t))[:, None] * inv[None, :]
    cos_r, sin_r = tile4(jnp.cos(ang_r)), tile4(jnp.sin(ang_r))
    cos_b, sin_b = tile4(jnp.cos(ang_b))[:, None, :], tile4(jnp.sin(ang_b))[:, None, :]
    half = RET_QK_HEAD // 2
    sign = jnp.asarray(np.tile(np.repeat([-1.0, 1.0], half), 2)[None, :], F32)

    log_g = np.log1p(-np.exp2(-5.0 - np.arange(RET_HEADS, dtype=np.float64)))
    i = np.arange(t)
    ci = i // CHUNK
    diff = (i[:, None] - i[None, :]).astype(np.float64)
    same = ci[:, None] == ci[None, :]
    earlier = ci[None, :] < ci[:, None]
    expo = np.where(same, np.abs(diff), diff)
    dmask = np.where((same | earlier)[None], np.exp(expo[None] * log_g[:, None, None]), 0.0)
    xi = np.exp((i + 1.0)[None, :] * log_g[:, None])
    zeta = np.exp((t - 1.0 - i)[None, :] * log_g[:, None])
    xi = np.broadcast_to(xi[:, :, None], (RET_HEADS, t, RET_V_HEAD))
    zeta = np.broadcast_to(zeta[:, :, None], (RET_HEADS, t, RET_V_HEAD))
    block_decay = [float(v) for v in np.exp(t * log_g)]
    return (cos_r, sin_r, cos_b, sin_b, sign, jnp.asarray(dmask, F32), jnp.asarray(xi, F32),
            jnp.asarray(zeta, F32), block_decay)


def _swap_halves(x):
    n = x.shape[1]
    half = RET_QK_HEAD // 2
    fwd = pltpu.roll(x, half, 1)
    bwd = pltpu.roll(x, n - half, 1)
    lane = lax.broadcasted_iota(jnp.int32, x.shape, 1)
    return jnp.where((lane & (RET_QK_HEAD - 1)) < half, bwd, fwd)


def _retention_block(block_decay, qk_ref, vz_ref, cosr_ref, sinr_ref, cosb_ref, sinb_ref,
                     sign_ref, dm_ref, xi_ref, zeta_ref, o_ref, state_ref):
    cb = cosb_ref[...]
    sb = sinb_ref[...]
    cos2 = cb * cosr_ref[...] - sb * sinr_ref[...]
    sin2 = (sb * cosr_ref[...] + cb * sinr_ref[...]) * sign_ref[...]
    reps = RET_QK // cos2.shape[1]
    cos = jnp.concatenate([cos2] * reps, axis=1)
    sin = jnp.concatenate([sin2] * reps, axis=1)
    q = qk_ref[:, :RET_QK].astype(F32)
    k = qk_ref[:, RET_QK:].astype(F32)
    qr = (q * cos + _swap_halves(q) * sin).astype(BF16)
    kr = ((k * cos + _swap_halves(k) * sin) * (RET_QK_HEAD ** -0.5)).astype(BF16)
    for h in range(RET_HEADS):
        qs = slice(h * RET_QK_HEAD, (h + 1) * RET_QK_HEAD)
        vs = slice(h * RET_V_HEAD, (h + 1) * RET_V_HEAD)
        qh = qr[:, qs]
        kh = kr[:, qs]
        vh = vz_ref[:, vs]
        zs = slice(RET_WIDTH + h * RET_V_HEAD, RET_WIDTH + (h + 1) * RET_V_HEAD)
        s = lax.dot_general(qh, kh, (((1,), (1,)), ((), ())), preferred_element_type=F32)
        s = s * dm_ref[h]
        inner = _dot(s.astype(BF16), vh)
        st = state_ref[h]
        cross = _dot(qh, st.astype(BF16)) * xi_ref[h]
        vz = (vh.astype(F32) * zeta_ref[h]).astype(BF16)
        kv = lax.dot_general(kh, vz, (((0,), (0,)), ((), ())), preferred_element_type=F32)
        state_ref[h] = st * block_decay[h] + kv
        o = inner + cross
        o = o * lax.rsqrt(jnp.mean(o * o, axis=-1, keepdims=True) + EPS)
        o_ref[:, vs] = (o * jax.nn.silu(vz_ref[:, zs].astype(F32))).astype(o_ref.dtype)


def _merge_block(ys_ref, yr_ref, gl_ref, x_ref, ws_ref, wr_ref, wo_ref, g_ref, o_ref):
    d = o_ref.shape[1]
    a = _dot(ys_ref[...], ws_ref[...])
    b = _dot(yr_ref[...], wr_ref[...])
    gs = gl_ref[:, :d].astype(F32)
    gr = gl_ref[:, d:].astype(F32)
    m = jax.nn.sigmoid(gs) * a + jax.nn.sigmoid(gr) * b
    o = _dot(m.astype(BF16), wo_ref[...])
    o_ref[...] = x_ref[...] + _rms_rows(o, g_ref[...])


def _tail_kernel(block_decay,
                 yssm_ref, zs_ref, qk_ref, vz_ref, cosb_ref, sinb_ref, gl_ref, x_ref,
                 wglu_ref, bglu_ref, cosr_ref, sinr_ref, sign_ref, dm_ref, xi_ref, zeta_ref,
                 ws_ref, wr_ref, wo_ref, g_ref,
                 o_ref, state_ref, ys_scr, yr_scr):
    @pl.when(pl.program_id(0) == 0)
    def _():
        state_ref[...] = jnp.zeros_like(state_ref)
        ys_scr[...] = jnp.zeros_like(ys_scr)
        yr_scr[...] = jnp.zeros_like(yr_scr)

    _merge_block(ys_scr, yr_scr, gl_ref, x_ref, ws_ref, wr_ref, wo_ref, g_ref, o_ref)
    _s5post_block(yssm_ref, zs_ref, wglu_ref, bglu_ref, ys_scr)
    _retention_block(block_decay, qk_ref, vz_ref, cosr_ref, sinr_ref, cosb_ref, sinb_ref,
                     sign_ref, dm_ref, xi_ref, zeta_ref, yr_scr, state_ref)


def _tail(y_ssm, proj, x2, w_glu, b_glu, w_s, w_r, w_o, gain):
    seq, d = x2.shape
    t = RET_T
    nblk = seq // t
    cos_r, sin_r, cos_b, sin_b, sign, dmask, xi, zeta, block_decay = _ret_tables(seq)
    cur = lambda s: jnp.minimum(s, nblk - 1)
    prv = lambda s: jnp.maximum(s - 1, 0)
    single = pl.Buffered(1)
    const = lambda a: pl.BlockSpec(a.shape, lambda s: (0,) * a.ndim, pipeline_mode=single)
    b_glu2 = b_glu.reshape(1, S5_WIDTH).astype(F32)
    gain2 = gain.reshape(1, d).astype(F32)
    return pl.pallas_call(
        functools.partial(_tail_kernel, block_decay),
        grid=(nblk + 1,),
        in_specs=[
            pl.BlockSpec((t, S5_WIDTH), lambda s: (cur(s), 0)),
            pl.BlockSpec((t, S5_WIDTH), lambda s: (cur(s), _ZS_OFF // S5_WIDTH)),
            pl.BlockSpec((t, 2 * RET_QK), lambda s: (cur(s), _QK_OFF // (2 * RET_QK))),
            pl.BlockSpec((t, 2 * RET_WIDTH), lambda s: (cur(s), _VZ_OFF // (2 * RET_WIDTH))),
            pl.BlockSpec((None, 1, cos_b.shape[2]), lambda s: (cur(s), 0, 0)),
            pl.BlockSpec((None, 1, sin_b.shape[2]), lambda s: (cur(s), 0, 0)),
            pl.BlockSpec((t, 2 * d), lambda s: (prv(s), _GL_OFF // (2 * d))),
            pl.BlockSpec((t, d), lambda s: (prv(s), 0)),
            const(w_glu), const(b_glu2), const(cos_r), const(sin_r), const(sign), const(dmask),
            const(xi), const(zeta), const(w_s), const(w_r), const(w_o), const(gain2),
        ],
        out_specs=pl.BlockSpec((t, d), lambda s: (prv(s), 0)),
        out_shape=jax.ShapeDtypeStruct((seq, d), F32),
        scratch_shapes=[
            pltpu.VMEM((RET_HEADS, RET_QK_HEAD, RET_V_HEAD), F32),
            pltpu.VMEM((t, S5_WIDTH), BF16),
            pltpu.VMEM((t, RET_WIDTH), BF16),
        ],
        compiler_params=pltpu.CompilerParams(
            dimension_semantics=("arbitrary",), vmem_limit_bytes=VMEM_LIMIT),
        name="tail",
    )(y_ssm, proj, proj, proj, cos_b, sin_b, proj, x2,
      w_glu, b_glu2, cos_r, sin_r, sign, dmask, xi, zeta, w_s, w_r, w_o, gain2)


def kernel(x, norm_pre, w_in, lam_re, lam_im, log_step, s5_b_re, s5_b_im, s5_c_re, s5_c_im, s5_d,
           w_glu, b_glu, w_proj_s5, w_proj_ret, w_out, norm_post):
    bsz, seq, d = x.shape
    assert bsz == 1 and d == D_MODEL and seq % 1024 == 0
    depth = w_in.shape[0]
    x2 = x.reshape(seq, d)
    for l in range(depth):
        w_regrouped = jnp.concatenate([w_in[l][:, a:b] for a, b in _PROJ_ORDER], axis=1).astype(BF16)
        proj, (w_glu_b, w_s_b, w_r_b, w_o_b) = _inproj(
            x2, norm_pre[l], w_regrouped, (w_glu[l], w_proj_s5[l], w_proj_ret[l], w_out[l]))
        tables = _s5_tables(lam_re[l], lam_im[l], log_step[l], s5_b_re[l], s5_b_im[l],
                            s5_c_re[l], s5_c_im[l], s5_d[l], seq // S5_T)
        y_ssm = _s5core(proj, tables)
        x2 = _tail(y_ssm, proj, x2, w_glu_b, b_glu[l], w_s_b, w_r_b, w_o_b, norm_post[l])
    return x2.reshape(bsz, seq, d)
```

```python
import functools
import math

import numpy as np
import jax
import jax.numpy as jnp
from jax import lax
from jax.experimental import pallas as pl
from jax.experimental.pallas import tpu as pltpu

F32 = jnp.float32
BF16 = jnp.bfloat16
HIGHEST = lax.Precision.HIGHEST

D_MODEL = 2048
EPS = 1e-6
CHUNK = 64

S5_WIDTH = 1024
S5_GROUP = 16
S5_GROUPS = 64
S5_STATE = 64

RET_HEADS = 8
RET_V_HEAD = 128
RET_QK_HEAD = 64
RET_QK = 512
RET_WIDTH = 1024
ROPE_BASE = 10000.0

_Q_BLK512 = (2 * S5_WIDTH) // 512
_K_BLK512 = (2 * S5_WIDTH + RET_QK) // 512
_ZS_BLK = 1
_V_BLK = (2 * S5_WIDTH + 2 * RET_QK) // 1024
_ZR_BLK = _V_BLK + 1
_GLS_BLK = _ZR_BLK + 1
_GLR_BLK = _GLS_BLK + 2

VMEM_LIMIT = 56 * 1024 * 1024
INPROJ_VMEM_LIMIT = 60 * 1024 * 1024
INPROJ_PIECE = 1280
INPROJ_ROWS = 128
SUBLANES = 8
LANES = 128

S5_T = 32
S5_GB = 8
S5_TW = S5_T * S5_GROUP
S5_RB = 4096
S5_SCAN_ROWS = 16
S5_BATCH = 4

RET_T = 256


def _dot(a, b):
    return jnp.dot(a, b, preferred_element_type=F32)


def _rms_rows(x, gain):
    ms = jnp.mean(x * x, axis=-1, keepdims=True)
    return x * lax.rsqrt(ms + EPS) * gain


def _inproj_kernel(n_extra, x_ref, g_ref, w_ref, *rest):
    extra_in = rest[:n_extra]
    o_ref = rest[n_extra]
    extra_out = rest[n_extra + 1:2 * n_extra + 1]
    h_ref = rest[2 * n_extra + 1]

    tm, tn = o_ref.shape
    pieces = [(lo, min(lo + INPROJ_PIECE, tn)) for lo in range(0, tn, INPROJ_PIECE)]

    @pl.when(pl.program_id(1) == 0)
    def _():
        for r0 in range(0, tm, INPROJ_ROWS):
            rows = slice(r0, r0 + INPROJ_ROWS)
            hc = _rms_rows(x_ref[rows, :], g_ref[...]).astype(BF16)
            h_ref[rows, :] = hc
            for lo, hi in pieces:
                o_ref[rows, lo:hi] = _dot(hc, w_ref[:, lo:hi]).astype(o_ref.dtype)

    @pl.when(pl.program_id(1) != 0)
    def _():
        for lo, hi in pieces:
            o_ref[:, lo:hi] = _dot(h_ref[...], w_ref[:, lo:hi]).astype(o_ref.dtype)

    for src, dst in zip(extra_in, extra_out):
        dst[...] = src[...].astype(dst.dtype)


def _inproj(x2, gain, w_bf16, later_weights, tm=1024, tn=2304):
    seq, d = x2.shape
    n = w_bf16.shape[1]
    grid = (seq // tm, n // tn)
    steps = grid[0] * grid[1]
    step_block = lambda i, j: (i * grid[1] + j, 0)
    extra_specs = [pl.BlockSpec((w.shape[0] // steps, w.shape[1]), step_block) for w in later_weights]
    outs = pl.pallas_call(
        functools.partial(_inproj_kernel, len(later_weights)),
        grid=grid,
        in_specs=[
            pl.BlockSpec((tm, d), lambda i, j: (i, 0)),
            pl.BlockSpec((1, d), lambda i, j: (0, 0)),
            pl.BlockSpec((d, tn), lambda i, j: (0, j)),
        ] + extra_specs,
        out_specs=[pl.BlockSpec((tm, tn), lambda i, j: (i, j))] + extra_specs,
        out_shape=[jax.ShapeDtypeStruct((seq, n), BF16)]
        + [jax.ShapeDtypeStruct(w.shape, BF16) for w in later_weights],
        scratch_shapes=[pltpu.VMEM((tm, d), BF16)],
        compiler_params=pltpu.CompilerParams(
            dimension_semantics=("arbitrary", "arbitrary"), vmem_limit_bytes=INPROJ_VMEM_LIMIT),
        name="inproj",
    )(x2, gain.reshape(1, d).astype(F32), w_bf16, *later_weights)
    return outs[0], outs[1:]


def _s5_tables(lam_re, lam_im, log_step, b_re, b_im, c_re, c_im, d_skip, n_chunks):
    t = S5_T
    lr = lam_re.astype(F32)
    li = lam_im.astype(F32)
    step = jnp.exp(log_step.astype(F32))[:, None]
    lrs = lr * step
    ang = li * step
    mag = jnp.exp(lrs)
    ab_re = mag * jnp.cos(ang)
    ab_im = mag * jnp.sin(ang)
    den = lr * lr + li * li
    nr = ab_re - 1.0
    f_re = (nr * lr + ab_im * li) / den
    f_im = (ab_im * lr - nr * li) / den
    br = b_re.astype(F32)
    bi = b_im.astype(F32)
    bb_re = f_re[..., None] * br - f_im[..., None] * bi
    bb_im = f_re[..., None] * bi + f_im[..., None] * br
    cr = c_re.astype(F32)
    ci = c_im.astype(F32)

    n_bits = int(math.log2(t))
    bit_re, bit_im = [ab_re], [ab_im]
    for _ in range(n_bits - 1):
        r, i = bit_re[-1], bit_im[-1]
        bit_re.append(r * r - i * i)
        bit_im.append(2.0 * r * i)
    pad_bits = ((0, 0), (0, 0), (0, SUBLANES - n_bits))
    abit_re = jnp.pad(jnp.stack(bit_re, axis=-1), pad_bits)
    abit_im = jnp.pad(jnp.stack(bit_im, axis=-1), pad_bits)
    bbt_re = jnp.tile(bb_re, (1, 1, LANES // S5_GROUP))
    bbt_im = jnp.tile(bb_im, (1, 1, LANES // S5_GROUP))
    e_ip = (jnp.arange(t, dtype=F32) + 1.0)[None, :, None]
    m_ip = jnp.exp(e_ip * lrs[:, None, :])
    pr = m_ip * jnp.cos(e_ip * ang[:, None, :])
    pi = m_ip * jnp.sin(e_ip * ang[:, None, :])
    p1 = jnp.concatenate([pr, -pi], axis=2)
    p2 = jnp.concatenate([-pi, -pr], axis=2)
    c_rr = jnp.concatenate([cr, cr], axis=2)
    c_ii = jnp.concatenate([ci, ci], axis=2)

    c_cat = jnp.concatenate([cr, -ci], axis=2)
    skip = jnp.broadcast_to(d_skip.astype(F32).reshape(S5_GROUPS, S5_GROUP, 1),
                            (S5_GROUPS, S5_GROUP, LANES))

    n_steps = max(1, int(math.ceil(math.log2(n_chunks))))
    m_t = jnp.exp(float(t) * lrs)
    sq_re, sq_im = [m_t * jnp.cos(float(t) * ang)], [m_t * jnp.sin(float(t) * ang)]
    for _ in range(n_steps - 1):
        r, i = sq_re[-1], sq_im[-1]
        sq_re.append(r * r - i * i)
        sq_im.append(2.0 * r * i)
    def pair_rows(sq):
        a = jnp.stack(sq, axis=0).reshape(n_steps, S5_GROUPS // 2, 2 * S5_STATE)
        a = jnp.transpose(a, (1, 0, 2))
        return jnp.pad(a, ((0, 0), (0, S5_SCAN_ROWS - n_steps), (0, 0)))

    per_group = (abit_re, abit_im, bbt_re, bbt_im, p1, p2, c_rr, c_ii, c_cat, skip)
    return per_group, pair_rows(sq_re), pair_rows(sq_im), n_steps


def _sublane_transpose(v):
    v = list(v)
    sub = lax.broadcasted_iota(jnp.int32, v[0].shape, 1)
    for k in (4, 2, 1):
        keep = (sub & k) == 0
        for r in range(SUBLANES):
            if r & k:
                continue
            a, b = v[r], v[r | k]
            v[r] = jnp.where(keep, a, pltpu.roll(b, k, 1))
            v[r | k] = jnp.where(keep, pltpu.roll(a, SUBLANES - k, 1), b)
    return v


def _s5core_kernel(n_steps, u_ref, abr_ref, abi_ref, bbr_ref, bbi_ref, p1_ref, p2_ref, crr_ref, cii_ref,
                   ccat_ref, skip_ref, apr_ref, api_ref, o_ref, rhs_scr, mt_scr, yt_scr):
    t = S5_T
    nch = rhs_scr.shape[2]
    nq = t // SUBLANES
    rb = min(S5_RB, nch * t)
    nb = rb // t
    nsub = nb // SUBLANES
    n_blocks = (nch * t) // rb

    for blk in range(n_blocks):
        x = u_ref[blk * rb:(blk + 1) * rb, :].astype(F32).reshape(nsub, SUBLANES, nq, SUBLANES, LANES)
        cols = slice(blk * nb, (blk + 1) * nb)
        for q in range(nq):
            w = _sublane_transpose([x[:, r, q] for r in range(SUBLANES)])
            for s in range(SUBLANES):
                i = q * SUBLANES + s
                rhs_scr[:, i * S5_GROUP:(i + 1) * S5_GROUP, cols] = (
                    w[s].reshape(nb, LANES).T.astype(BF16).reshape(S5_GB, S5_GROUP, nb))

    tail_lane = lax.broadcasted_iota(jnp.int32, (S5_GROUP, LANES), 1)
    tail_row = lax.broadcasted_iota(jnp.int32, (S5_GROUP, LANES), 0)
    lag0 = tail_lane == tail_row + (LANES - S5_GROUP)
    zero_half = jnp.zeros((S5_GROUP, S5_TW), F32)
    per_tile = LANES // S5_GROUP
    chunk_row = lax.broadcasted_iota(jnp.int32, (nch, LANES), 0)

    def shift_down(v, k):
        if k % SUBLANES == 0:
            return jnp.concatenate([jnp.zeros((k, LANES), F32), v[:nch - k]], axis=0)
        return jnp.where(chunk_row >= k, pltpu.roll(v, k, 0), 0.0)

    lane_expo = (t - 1) - lax.shift_right_logical(
        lax.broadcasted_iota(jnp.int32, (S5_STATE, S5_TW), 1), int(math.log2(S5_GROUP)))
    lane_reps = S5_TW // LANES

    def state_weights(g):
        abr = abr_ref[g]
        abi = abi_ref[g]
        pw_re = pw_im = None
        for b in range(int(math.log2(t))):
            bit = (lax.shift_right_logical(lane_expo, b) & 1) == 1
            f_re = jnp.where(bit, abr[:, b:b + 1], 1.0)
            f_im = jnp.where(bit, abi[:, b:b + 1], 0.0)
            if pw_re is None:
                pw_re, pw_im = f_re, f_im
            else:
                pw_re, pw_im = pw_re * f_re - pw_im * f_im, pw_re * f_im + pw_im * f_re
        br = jnp.concatenate([bbr_ref[g]] * lane_reps, axis=1)
        bi = jnp.concatenate([bbi_ref[g]] * lane_reps, axis=1)
        return jnp.concatenate([pw_re * br - pw_im * bi, pw_re * bi + pw_im * br], axis=0)

    def output_weights(g):
        crr = crr_ref[g]
        cii = cii_ref[g]
        p1 = p1_ref[g]
        p2 = p2_ref[g]
        return jnp.concatenate(
            [(crr * p1[i:i + 1, :] + cii * p2[i:i + 1, :]).astype(BF16) for i in range(t)], axis=0)

    def local_products(g, slot):
        mt = mt_scr.at[slot]
        ws = state_weights(g)
        strip = jnp.dot(ccat_ref[g], ws, precision=HIGHEST, preferred_element_type=F32)
        tail = strip[:, S5_TW - LANES:] + jnp.where(lag0, skip_ref[g], 0.0)
        full = jnp.concatenate([strip[:, :S5_TW - LANES], tail, zero_half], axis=1)
        rot = [full] + [pltpu.roll(full, 2 * S5_TW - S5_GROUP * r, 1) for r in range(1, per_tile)]
        for i in range(t):
            a, r = divmod(t - 1 - i, per_tile)
            mt[i * S5_GROUP:(i + 1) * S5_GROUP, :] = (
                rot[r][:, a * LANES:a * LANES + S5_TW].astype(BF16))
        rhs = rhs_scr[g]
        yt_scr[g] = _dot(mt[...], rhs)
        return _dot(ws.astype(BF16), rhs)

    def scan_pair(gp, g0, s0, s1):
        xr = jnp.concatenate([s0[:S5_STATE], s1[:S5_STATE]], axis=0).T
        xi = jnp.concatenate([s0[S5_STATE:], s1[S5_STATE:]], axis=0).T
        apr = apr_ref[gp]
        api = api_ref[gp]
        for k in range(n_steps):
            ar = apr[k:k + 1, :]
            ai = api[k:k + 1, :]
            sr = shift_down(xr, 1 << k)
            si = shift_down(xi, 1 << k)
            xr, xi = xr + (ar * sr - ai * si), xi + (ar * si + ai * sr)
        pr = shift_down(xr, 1).T
        pi = shift_down(xi, 1).T
        for j in range(2):
            rows = slice(j * S5_STATE, (j + 1) * S5_STATE)
            xprev = jnp.concatenate([pr[rows], pi[rows]], axis=0).astype(BF16)
            yt_scr[g0 + j] += _dot(output_weights(g0 + j), xprev)

    def per_batch(b, carry):
        g0 = S5_BATCH * b
        s = [local_products(g0 + j, j) for j in range(S5_BATCH)]
        for p in range(S5_BATCH // 2):
            scan_pair((S5_BATCH // 2) * b + p, g0 + 2 * p, s[2 * p], s[2 * p + 1])
        return carry

    lax.fori_loop(0, S5_GB // S5_BATCH, per_batch, 0)

    for blk in range(n_blocks):
        cols = slice(blk * nb, (blk + 1) * nb)
        per_q = []
        for q in range(nq):
            w = []
            for s in range(SUBLANES):
                i = q * SUBLANES + s
                piece = yt_scr[:, i * S5_GROUP:(i + 1) * S5_GROUP, cols].reshape(S5_GB * S5_GROUP, nb)
                w.append(piece.T.reshape(nsub, SUBLANES, LANES))
            per_q.append(_sublane_transpose(w))
        x = jnp.stack([jnp.stack([per_q[q][r] for q in range(nq)], axis=1)
                       for r in range(SUBLANES)], axis=1)
        o_ref[blk * rb:(blk + 1) * rb, :] = x.reshape(rb, LANES).astype(o_ref.dtype)


def _s5core(proj, tables):
    seq = proj.shape[0]
    per_group, ap_re, ap_im, n_steps = tables
    nch = seq // S5_T
    lanes = S5_GB * S5_GROUP
    grp = lambda i: (i, 0, 0)
    gspec = lambda a: pl.BlockSpec((S5_GB,) + a.shape[1:], grp)
    pspec = lambda a: pl.BlockSpec((S5_GB // 2,) + a.shape[1:], grp)
    return pl.pallas_call(
        functools.partial(_s5core_kernel, n_steps),
        grid=(S5_GROUPS // S5_GB,),
        in_specs=([pl.BlockSpec((seq, lanes), lambda i: (0, i))] + [gspec(a) for a in per_group]
                  + [pspec(ap_re), pspec(ap_im)]),
        out_specs=pl.BlockSpec((seq, lanes), lambda i: (0, i)),
        out_shape=jax.ShapeDtypeStruct((seq, S5_WIDTH), BF16),
        scratch_shapes=[
            pltpu.VMEM((S5_GB, S5_TW, nch), BF16),
            pltpu.VMEM((S5_BATCH, S5_TW, S5_TW), BF16),
            pltpu.VMEM((S5_GB, S5_TW, nch), F32),
        ],
        compiler_params=pltpu.CompilerParams(
            dimension_semantics=("parallel",), vmem_limit_bytes=VMEM_LIMIT),
        name="s5_core",
    )(proj, *per_group, ap_re, ap_im)


def _gelu_tanh(y):
    return 0.5 * y * (1.0 + jnp.tanh(math.sqrt(2.0 / math.pi) * (y + 0.044715 * (y * y * y))))


def _s5post_block(y_ref, z_ref, wglu_ref, bglu_ref, o_ref):
    g = _gelu_tanh(y_ref[...].astype(F32))
    gate = jax.nn.sigmoid(_dot(g.astype(BF16), wglu_ref[...]) + bglu_ref[...])
    o_ref[...] = (g * gate * jax.nn.silu(z_ref[...].astype(F32))).astype(o_ref.dtype)


def _ret_tables(seq):
    t = RET_T
    inv = ROPE_BASE ** (-jnp.arange(0, RET_QK_HEAD, 2, dtype=F32) / RET_QK_HEAD)
    tile4 = lambda a: jnp.concatenate([a, a, a, a], axis=-1)
    ang_r = jnp.arange(t, dtype=F32)[:, None] * inv[None, :]
    ang_b = (jnp.arange(seq // t, dtype=F32) * float(t))[:, None] * inv[None, :]
    cos_r, sin_r = tile4(jnp.cos(ang_r)), tile4(jnp.sin(ang_r))
    cos_b, sin_b = tile4(jnp.cos(ang_b))[:, None, :], tile4(jnp.sin(ang_b))[:, None, :]
    half = RET_QK_HEAD // 2
    sign = jnp.asarray(np.tile(np.repeat([-1.0, 1.0], half), 2)[None, :], F32)

    log_g = np.log1p(-np.exp2(-5.0 - np.arange(RET_HEADS, dtype=np.float64)))
    i = np.arange(t)
    ci = i // CHUNK
    diff = (i[:, None] - i[None, :]).astype(np.float64)
    same = ci[:, None] == ci[None, :]
    earlier = ci[None, :] < ci[:, None]
    expo = np.where(same, np.abs(diff), diff)
    dmask = np.where((same | earlier)[None], np.exp(expo[None] * log_g[:, None, None]), 0.0)
    xi = np.exp((i + 1.0)[None, :] * log_g[:, None])
    zeta = np.exp((t - 1.0 - i)[None, :] * log_g[:, None])
    xi = np.broadcast_to(xi[:, :, None], (RET_HEADS, t, RET_V_HEAD))
    zeta = np.repeat(zeta.T, RET_QK_HEAD, axis=1)
    block_decay = [float(v) for v in np.exp(t * log_g)]
    return (cos_r, sin_r, cos_b, sin_b, sign, jnp.asarray(dmask, F32), jnp.asarray(xi, F32),
            jnp.asarray(zeta, F32), block_decay)


def _swap_halves(x):
    n = x.shape[1]
    half = RET_QK_HEAD // 2
    fwd = pltpu.roll(x, half, 1)
    bwd = pltpu.roll(x, n - half, 1)
    lane = lax.broadcasted_iota(jnp.int32, x.shape, 1)
    return jnp.where((lane & (RET_QK_HEAD - 1)) < half, bwd, fwd)


def _retention_block(block_decay, q_ref, k_ref, v_ref, z_ref, cosr_ref, sinr_ref, cosb_ref, sinb_ref,
                     sign_ref, dm_ref, xi_ref, zeta_ref, o_ref, state_ref):
    cb = cosb_ref[...]
    sb = sinb_ref[...]
    cos2 = cb * cosr_ref[...] - sb * sinr_ref[...]
    sin2 = (sb * cosr_ref[...] + cb * sinr_ref[...]) * sign_ref[...]
    pair_w = cos2.shape[1]
    per_pair = pair_w // RET_QK_HEAD
    for h in range(RET_HEADS):
        if h % per_pair == 0:
            cols = slice((h // per_pair) * pair_w, (h // per_pair + 1) * pair_w)
            q = q_ref[:, cols].astype(F32)
            k = k_ref[:, cols].astype(F32)
            qr = (q * cos2 + _swap_halves(q) * sin2).astype(BF16)
            ks = (k * cos2 + _swap_halves(k) * sin2) * (RET_QK_HEAD ** -0.5)
            kr = ks.astype(BF16)
            kz = (ks * zeta_ref[:, cols]).astype(BF16)
        qs = slice((h % per_pair) * RET_QK_HEAD, (h % per_pair + 1) * RET_QK_HEAD)
        vs = slice(h * RET_V_HEAD, (h + 1) * RET_V_HEAD)
        qh = qr[:, qs]
        kh = kr[:, qs]
        vh = v_ref[:, vs]
        s = lax.dot_general(qh, kh, (((1,), (1,)), ((), ())), preferred_element_type=F32)
        s = s * dm_ref[h]
        inner = _dot(s.astype(BF16), vh)
        st = state_ref[h]
        cross = _dot(qh, st.astype(BF16)) * xi_ref[h]
        kv = lax.dot_general(kz[:, qs], vh, (((0,), (0,)), ((), ())), preferred_element_type=F32)
        state_ref[h] = st * block_decay[h] + kv
        o = inner + cross
        o = o * lax.rsqrt(jnp.mean(o * o, axis=-1, keepdims=True) + EPS)
        o_ref[:, vs] = (o * jax.nn.silu(z_ref[:, vs].astype(F32))).astype(o_ref.dtype)


def _merge_block(ys_ref, yr_ref, gs0_ref, gs1_ref, gr0_ref, gr1_ref, x_ref, ws_ref, wr_ref, wo_ref,
                 g_ref, o_ref):
    a = _dot(ys_ref[...], ws_ref[...])
    b = _dot(yr_ref[...], wr_ref[...])
    gs = jnp.concatenate([gs0_ref[...], gs1_ref[...]], axis=1).astype(F32)
    gr = jnp.concatenate([gr0_ref[...], gr1_ref[...]], axis=1).astype(F32)
    m = jax.nn.sigmoid(gs) * a + jax.nn.sigmoid(gr) * b
    o = _dot(m.astype(BF16), wo_ref[...])
    o_ref[...] = x_ref[...] + _rms_rows(o, g_ref[...])


def _tail_kernel(block_decay,
                 yssm_ref, zs_ref, q_ref, k_ref, v_ref, zr_ref, cosb_ref, sinb_ref,
                 gs0_ref, gs1_ref, gr0_ref, gr1_ref, x_ref,
                 wglu_ref, bglu_ref, cosr_ref, sinr_ref, sign_ref, dm_ref, xi_ref, zeta_ref,
                 ws_ref, wr_ref, wo_ref, g_ref,
                 o_ref, state_ref, ys_scr, yr_scr):
    @pl.when(pl.program_id(0) == 0)
    def _():
        state_ref[...] = jnp.zeros_like(state_ref)
        ys_scr[...] = jnp.zeros_like(ys_scr)
        yr_scr[...] = jnp.zeros_like(yr_scr)

    _merge_block(ys_scr, yr_scr, gs0_ref, gs1_ref, gr0_ref, gr1_ref, x_ref, ws_ref, wr_ref, wo_ref,
                 g_ref, o_ref)
    _s5post_block(yssm_ref, zs_ref, wglu_ref, bglu_ref, ys_scr)
    _retention_block(block_decay, q_ref, k_ref, v_ref, zr_ref, cosr_ref, sinr_ref, cosb_ref, sinb_ref,
                     sign_ref, dm_ref, xi_ref, zeta_ref, yr_scr, state_ref)


def _tail(y_ssm, proj, x2, w_glu, b_glu, w_s, w_r, w_o, gain):
    seq, d = x2.shape
    t = RET_T
    nblk = seq // t
    cos_r, sin_r, cos_b, sin_b, sign, dmask, xi, zeta, block_decay = _ret_tables(seq)
    blk = 1024
    cur = lambda s: jnp.minimum(s, nblk - 1)
    prv = lambda s: jnp.maximum(s - 1, 0)
    single = pl.Buffered(1)
    const = lambda a: pl.BlockSpec(a.shape, lambda s: (0,) * a.ndim, pipeline_mode=single)
    b_glu2 = b_glu.reshape(1, S5_WIDTH).astype(F32)
    gain2 = gain.reshape(1, d).astype(F32)
    return pl.pallas_call(
        functools.partial(_tail_kernel, block_decay),
        grid=(nblk + 1,),
        in_specs=[
            pl.BlockSpec((t, S5_WIDTH), lambda s: (cur(s), 0)),
            pl.BlockSpec((t, S5_WIDTH), lambda s: (cur(s), _ZS_BLK)),
            pl.BlockSpec((t, RET_QK), lambda s: (cur(s), _Q_BLK512)),
            pl.BlockSpec((t, RET_QK), lambda s: (cur(s), _K_BLK512)),
            pl.BlockSpec((t, RET_WIDTH), lambda s: (cur(s), _V_BLK)),
            pl.BlockSpec((t, RET_WIDTH), lambda s: (cur(s), _ZR_BLK)),
            pl.BlockSpec((None, 1, cos_b.shape[2]), lambda s: (cur(s), 0, 0)),
            pl.BlockSpec((None, 1, sin_b.shape[2]), lambda s: (cur(s), 0, 0)),
            pl.BlockSpec((t, blk), lambda s: (prv(s), _GLS_BLK)),
            pl.BlockSpec((t, blk), lambda s: (prv(s), _GLS_BLK + 1)),
            pl.BlockSpec((t, blk), lambda s: (prv(s), _GLR_BLK)),
            pl.BlockSpec((t, blk), lambda s: (prv(s), _GLR_BLK + 1)),
            pl.BlockSpec((t, d), lambda s: (prv(s), 0)),
            const(w_glu), const(b_glu2), const(cos_r), const(sin_r), const(sign), const(dmask),
            const(xi), const(zeta), const(w_s), const(w_r), const(w_o), const(gain2),
        ],
        out_specs=pl.BlockSpec((t, d), lambda s: (prv(s), 0)),
        out_shape=jax.ShapeDtypeStruct((seq, d), F32),
        scratch_shapes=[
            pltpu.VMEM((RET_HEADS, RET_QK_HEAD, RET_V_HEAD), F32),
            pltpu.VMEM((t, S5_WIDTH), BF16),
            pltpu.VMEM((t, RET_WIDTH), BF16),
        ],
        compiler_params=pltpu.CompilerParams(
            dimension_semantics=("arbitrary",), vmem_limit_bytes=VMEM_LIMIT),
        name="tail",
    )(y_ssm, proj, proj, proj, proj, proj, cos_b, sin_b, proj, proj, proj, proj, x2,
      w_glu, b_glu2, cos_r, sin_r, sign, dmask, xi, zeta, w_s, w_r, w_o, gain2)


def kernel(x, norm_pre, w_in, lam_re, lam_im, log_step, s5_b_re, s5_b_im, s5_c_re, s5_c_im, s5_d,
           w_glu, b_glu, w_proj_s5, w_proj_ret, w_out, norm_post):
    bsz, seq, d = x.shape
    assert bsz == 1 and d == D_MODEL and seq % 1024 == 0
    depth = w_in.shape[0]
    x2 = x.reshape(seq, d)
    for l in range(depth):
        proj, (w_glu_b, w_s_b, w_r_b, w_o_b) = _inproj(
            x2, norm_pre[l], w_in[l].astype(BF16), (w_glu[l], w_proj_s5[l], w_proj_ret[l], w_out[l]))
        tables = _s5_tables(lam_re[l], lam_im[l], log_step[l], s5_b_re[l], s5_b_im[l],
                            s5_c_re[l], s5_c_im[l], s5_d[l], seq // S5_T)
        y_ssm = _s5core(proj, tables)
        x2 = _tail(y_ssm, proj, x2, w_glu_b, b_glu[l], w_s_b, w_r_b, w_o_b, norm_post[l])
    return x2.reshape(bsz, seq, d)
```

```python
import functools
import math

import numpy as np
import jax
import jax.numpy as jnp
from jax import lax
from jax.experimental import pallas as pl
from jax.experimental.pallas import tpu as pltpu

F32 = jnp.float32
BF16 = jnp.bfloat16
HIGHEST = lax.Precision.HIGHEST

D_MODEL = 2048
EPS = 1e-6
CHUNK = 64

S5_WIDTH = 1024
S5_GROUP = 16
S5_GROUPS = 64
S5_STATE = 64

RET_HEADS = 8
RET_V_HEAD = 128
RET_QK_HEAD = 64
RET_QK = 512
RET_WIDTH = 1024
ROPE_BASE = 10000.0

_Q_BLK512 = (2 * S5_WIDTH) // 512
_K_BLK512 = (2 * S5_WIDTH + RET_QK) // 512
_ZS_BLK = 1
_V_BLK = (2 * S5_WIDTH + 2 * RET_QK) // 1024
_ZR_BLK = _V_BLK + 1
_GLS_BLK = _ZR_BLK + 1
_GLR_BLK = _GLS_BLK + 2

VMEM_LIMIT = 56 * 1024 * 1024
INPROJ_VMEM_LIMIT = 60 * 1024 * 1024
INPROJ_PIECE = 1280
INPROJ_ROWS = 128
SUBLANES = 8
LANES = 128

S5_T = 32
S5_GB = 8
S5_TW = S5_T * S5_GROUP
S5_RB = 4096
S5_SCAN_ROWS = 16
S5_BATCH = 4

RET_T = 256


def _dot(a, b):
    return jnp.dot(a, b, preferred_element_type=F32)


def _rms_rows(x, gain):
    ms = jnp.mean(x * x, axis=-1, keepdims=True)
    return x * lax.rsqrt(ms + EPS) * gain


def _inproj_kernel(n_extra, x_ref, g_ref, w_ref, *rest):
    extra_in = rest[:n_extra]
    o_ref = rest[n_extra]
    extra_out = rest[n_extra + 1:2 * n_extra + 1]
    h_ref = rest[2 * n_extra + 1]

    tm, tn = o_ref.shape
    pieces = [(lo, min(lo + INPROJ_PIECE, tn)) for lo in range(0, tn, INPROJ_PIECE)]

    @pl.when(pl.program_id(1) == 0)
    def _():
        for r0 in range(0, tm, INPROJ_ROWS):
            rows = slice(r0, r0 + INPROJ_ROWS)
            hc = _rms_rows(x_ref[rows, :], g_ref[...]).astype(BF16)
            h_ref[rows, :] = hc
            for lo, hi in pieces:
                o_ref[rows, lo:hi] = _dot(hc, w_ref[:, lo:hi]).astype(o_ref.dtype)

    @pl.when(pl.program_id(1) != 0)
    def _():
        for lo, hi in pieces:
            o_ref[:, lo:hi] = _dot(h_ref[...], w_ref[:, lo:hi]).astype(o_ref.dtype)

    for src, dst in zip(extra_in, extra_out):
        dst[...] = src[...].astype(dst.dtype)


def _inproj(x2, gain, w_bf16, later_weights, tm=1024, tn=2304):
    seq, d = x2.shape
    n = w_bf16.shape[1]
    grid = (seq // tm, n // tn)
    steps = grid[0] * grid[1]
    step_block = lambda i, j: (i * grid[1] + j, 0)
    extra_specs = [pl.BlockSpec((w.shape[0] // steps, w.shape[1]), step_block) for w in later_weights]
    outs = pl.pallas_call(
        functools.partial(_inproj_kernel, len(later_weights)),
        grid=grid,
        in_specs=[
            pl.BlockSpec((tm, d), lambda i, j: (i, 0)),
            pl.BlockSpec((1, d), lambda i, j: (0, 0)),
            pl.BlockSpec((d, tn), lambda i, j: (0, j)),
        ] + extra_specs,
        out_specs=[pl.BlockSpec((tm, tn), lambda i, j: (i, j))] + extra_specs,
        out_shape=[jax.ShapeDtypeStruct((seq, n), BF16)]
        + [jax.ShapeDtypeStruct(w.shape, BF16) for w in later_weights],
        scratch_shapes=[pltpu.VMEM((tm, d), BF16)],
        compiler_params=pltpu.CompilerParams(
            dimension_semantics=("arbitrary", "arbitrary"), vmem_limit_bytes=INPROJ_VMEM_LIMIT),
        name="inproj",
    )(x2, gain.reshape(1, d).astype(F32), w_bf16, *later_weights)
    return outs[0], outs[1:]


def _s5_tables(lam_re, lam_im, log_step, b_re, b_im, c_re, c_im, d_skip, n_chunks):
    t = S5_T
    lr = lam_re.astype(F32)
    li = lam_im.astype(F32)
    step = jnp.exp(log_step.astype(F32))[:, None]
    lrs = lr * step
    ang = li * step
    mag = jnp.exp(lrs)
    ab_re = mag * jnp.cos(ang)
    ab_im = mag * jnp.sin(ang)
    den = lr * lr + li * li
    nr = ab_re - 1.0
    f_re = (nr * lr + ab_im * li) / den
    f_im = (ab_im * lr - nr * li) / den
    br = b_re.astype(F32)
    bi = b_im.astype(F32)
    bb_re = f_re[..., None] * br - f_im[..., None] * bi
    bb_im = f_re[..., None] * bi + f_im[..., None] * br
    cr = c_re.astype(F32)
    ci = c_im.astype(F32)

    n_bits = int(math.log2(t))
    bit_re, bit_im = [ab_re], [ab_im]
    for _ in range(n_bits - 1):
        r, i = bit_re[-1], bit_im[-1]
        bit_re.append(r * r - i * i)
        bit_im.append(2.0 * r * i)
    pad_bits = ((0, 0), (0, 0), (0, SUBLANES - n_bits))
    abit_re = jnp.pad(jnp.stack(bit_re, axis=-1), pad_bits)
    abit_im = jnp.pad(jnp.stack(bit_im, axis=-1), pad_bits)
    bbt_re = jnp.tile(bb_re, (1, 1, LANES // S5_GROUP))
    bbt_im = jnp.tile(bb_im, (1, 1, LANES // S5_GROUP))
    e_ip = (jnp.arange(t, dtype=F32) + 1.0)[None, :, None]
    m_ip = jnp.exp(e_ip * lrs[:, None, :])
    pr = m_ip * jnp.cos(e_ip * ang[:, None, :])
    pi = m_ip * jnp.sin(e_ip * ang[:, None, :])
    p1 = jnp.concatenate([pr, -pi], axis=2)
    p2 = jnp.concatenate([-pi, -pr], axis=2)
    c_rr = jnp.concatenate([cr, cr], axis=2)
    c_ii = jnp.concatenate([ci, ci], axis=2)

    c_cat = jnp.concatenate([cr, -ci], axis=2)
    skip = jnp.broadcast_to(d_skip.astype(F32).reshape(S5_GROUPS, S5_GROUP, 1),
                            (S5_GROUPS, S5_GROUP, LANES))

    n_steps = max(1, int(math.ceil(math.log2(n_chunks))))
    m_t = jnp.exp(float(t) * lrs)
    sq_re, sq_im = [m_t * jnp.cos(float(t) * ang)], [m_t * jnp.sin(float(t) * ang)]
    for _ in range(n_steps - 1):
        r, i = sq_re[-1], sq_im[-1]
        sq_re.append(r * r - i * i)
        sq_im.append(2.0 * r * i)
    def pair_rows(sq):
        a = jnp.stack(sq, axis=0).reshape(n_steps, S5_GROUPS // 2, 2 * S5_STATE)
        a = jnp.transpose(a, (1, 0, 2))
        return jnp.pad(a, ((0, 0), (0, S5_SCAN_ROWS - n_steps), (0, 0)))

    per_group = (abit_re, abit_im, bbt_re, bbt_im, p1, p2, c_rr, c_ii, c_cat, skip)
    return per_group, pair_rows(sq_re), pair_rows(sq_im), n_steps


def _sublane_transpose(v):
    v = list(v)
    sub = lax.broadcasted_iota(jnp.int32, v[0].shape, 1)
    for k in (4, 2, 1):
        keep = (sub & k) == 0
        for r in range(SUBLANES):
            if r & k:
                continue
            a, b = v[r], v[r | k]
            v[r] = jnp.where(keep, a, pltpu.roll(b, k, 1))
            v[r | k] = jnp.where(keep, pltpu.roll(a, SUBLANES - k, 1), b)
    return v


def _s5core_kernel(n_steps, u_ref, abr_ref, abi_ref, bbr_ref, bbi_ref, p1_ref, p2_ref, crr_ref, cii_ref,
                   ccat_ref, skip_ref, apr_ref, api_ref, o_ref, rhs_scr, mt_scr, yt_scr):
    t = S5_T
    nch = rhs_scr.shape[2]
    nq = t // SUBLANES
    rb = min(S5_RB, nch * t)
    nb = rb // t
    nsub = nb // SUBLANES
    n_blocks = (nch * t) // rb

    for blk in range(n_blocks):
        x = u_ref[blk * rb:(blk + 1) * rb, :].astype(F32).reshape(nsub, SUBLANES, nq, SUBLANES, LANES)
        cols = slice(blk * nb, (blk + 1) * nb)
        for q in range(nq):
            w = _sublane_transpose([x[:, r, q] for r in range(SUBLANES)])
            for s in range(SUBLANES):
                i = q * SUBLANES + s
                rhs_scr[:, i * S5_GROUP:(i + 1) * S5_GROUP, cols] = (
                    w[s].reshape(nb, LANES).T.astype(BF16).reshape(S5_GB, S5_GROUP, nb))

    tail_lane = lax.broadcasted_iota(jnp.int32, (S5_GROUP, LANES), 1)
    tail_row = lax.broadcasted_iota(jnp.int32, (S5_GROUP, LANES), 0)
    lag0 = tail_lane == tail_row + (LANES - S5_GROUP)
    zero_half = jnp.zeros((S5_GROUP, S5_TW), F32)
    per_tile = LANES // S5_GROUP
    chunk_row = lax.broadcasted_iota(jnp.int32, (nch, LANES), 0)

    def shift_down(v, k):
        if k % SUBLANES == 0:
            return jnp.concatenate([jnp.zeros((k, LANES), F32), v[:nch - k]], axis=0)
        return jnp.where(chunk_row >= k, pltpu.roll(v, k, 0), 0.0)

    lane_expo = (t - 1) - lax.shift_right_logical(
        lax.broadcasted_iota(jnp.int32, (S5_STATE, S5_TW), 1), int(math.log2(S5_GROUP)))
    lane_reps = S5_TW // LANES

    def state_weights(g):
        abr = abr_ref[g]
        abi = abi_ref[g]
        pw_re = pw_im = None
        for b in range(int(math.log2(t))):
            bit = (lax.shift_right_logical(lane_expo, b) & 1) == 1
            f_re = jnp.where(bit, abr[:, b:b + 1], 1.0)
            f_im = jnp.where(bit, abi[:, b:b + 1], 0.0)
            if pw_re is None:
                pw_re, pw_im = f_re, f_im
            else:
                pw_re, pw_im = pw_re * f_re - pw_im * f_im, pw_re * f_im + pw_im * f_re
        br = jnp.concatenate([bbr_ref[g]] * lane_reps, axis=1)
        bi = jnp.concatenate([bbi_ref[g]] * lane_reps, axis=1)
        return jnp.concatenate([pw_re * br - pw_im * bi, pw_re * bi + pw_im * br], axis=0)

    def output_weights(g):
        crr = crr_ref[g]
        cii = cii_ref[g]
        p1 = p1_ref[g]
        p2 = p2_ref[g]
        return jnp.concatenate(
            [(crr * p1[i:i + 1, :] + cii * p2[i:i + 1, :]).astype(BF16) for i in range(t)], axis=0)

    def build_weights(g, slot):
        mt = mt_scr.at[slot]
        ws = state_weights(g)
        strip = jnp.dot(ccat_ref[g], ws, precision=HIGHEST, preferred_element_type=F32)
        tail = strip[:, S5_TW - LANES:] + jnp.where(lag0, skip_ref[g], 0.0)
        full = jnp.concatenate([strip[:, :S5_TW - LANES], tail, zero_half], axis=1)
        rot = [full] + [pltpu.roll(full, 2 * S5_TW - S5_GROUP * r, 1) for r in range(1, per_tile)]
        for i in range(t):
            a, r = divmod(t - 1 - i, per_tile)
            mt[i * S5_GROUP:(i + 1) * S5_GROUP, :] = (
                rot[r][:, a * LANES:a * LANES + S5_TW].astype(BF16))
        return ws.astype(BF16)

    def local_products(g, slot, ws_b):
        rhs = rhs_scr[g]
        yt_scr[g] = _dot(mt_scr[slot], rhs)
        return _dot(ws_b, rhs)

    def scan_pair(gp, g0, s0, s1):
        xr = jnp.concatenate([s0[:S5_STATE], s1[:S5_STATE]], axis=0).T
        xi = jnp.concatenate([s0[S5_STATE:], s1[S5_STATE:]], axis=0).T
        apr = apr_ref[gp]
        api = api_ref[gp]
        for k in range(n_steps):
            ar = apr[k:k + 1, :]
            ai = api[k:k + 1, :]
            sr = shift_down(xr, 1 << k)
            si = shift_down(xi, 1 << k)
            xr, xi = xr + (ar * sr - ai * si), xi + (ar * si + ai * sr)
        pr = shift_down(xr, 1).T
        pi = shift_down(xi, 1).T
        for j in range(2):
            rows = slice(j * S5_STATE, (j + 1) * S5_STATE)
            xprev = jnp.concatenate([pr[rows], pi[rows]], axis=0).astype(BF16)
            yt_scr[g0 + j] += _dot(output_weights(g0 + j), xprev)

    def per_batch(b, carry):
        g0 = S5_BATCH * b
        ws_b = [build_weights(g0, 0)]
        s = []
        for j in range(S5_BATCH):
            if j + 1 < S5_BATCH:
                ws_b.append(build_weights(g0 + j + 1, j + 1))
            s.append(local_products(g0 + j, j, ws_b[j]))
        for p in range(S5_BATCH // 2):
            scan_pair((S5_BATCH // 2) * b + p, g0 + 2 * p, s[2 * p], s[2 * p + 1])
        return carry

    lax.fori_loop(0, S5_GB // S5_BATCH, per_batch, 0)

    for blk in range(n_blocks):
        cols = slice(blk * nb, (blk + 1) * nb)
        per_q = []
        for q in range(nq):
            w = []
            for s in range(SUBLANES):
                i = q * SUBLANES + s
                piece = yt_scr[:, i * S5_GROUP:(i + 1) * S5_GROUP, cols].reshape(S5_GB * S5_GROUP, nb)
                w.append(piece.T.reshape(nsub, SUBLANES, LANES))
            per_q.append(_sublane_transpose(w))
        x = jnp.stack([jnp.stack([per_q[q][r] for q in range(nq)], axis=1)
                       for r in range(SUBLANES)], axis=1)
        o_ref[blk * rb:(blk + 1) * rb, :] = x.reshape(rb, LANES).astype(o_ref.dtype)


def _s5core(proj, tables):
    seq = proj.shape[0]
    per_group, ap_re, ap_im, n_steps = tables
    nch = seq // S5_T
    lanes = S5_GB * S5_GROUP
    grp = lambda i: (i, 0, 0)
    gspec = lambda a: pl.BlockSpec((S5_GB,) + a.shape[1:], grp)
    pspec = lambda a: pl.BlockSpec((S5_GB // 2,) + a.shape[1:], grp)
    return pl.pallas_call(
        functools.partial(_s5core_kernel, n_steps),
        grid=(S5_GROUPS // S5_GB,),
        in_specs=([pl.BlockSpec((seq, lanes), lambda i: (0, i))] + [gspec(a) for a in per_group]
                  + [pspec(ap_re), pspec(ap_im)]),
        out_specs=pl.BlockSpec((seq, lanes), lambda i: (0, i)),
        out_shape=jax.ShapeDtypeStruct((seq, S5_WIDTH), BF16),
        scratch_shapes=[
            pltpu.VMEM((S5_GB, S5_TW, nch), BF16),
            pltpu.VMEM((S5_BATCH, S5_TW, S5_TW), BF16),
            pltpu.VMEM((S5_GB, S5_TW, nch), F32),
        ],
        compiler_params=pltpu.CompilerParams(
            dimension_semantics=("parallel",), vmem_limit_bytes=VMEM_LIMIT),
        name="s5_core",
    )(proj, *per_group, ap_re, ap_im)


def _gelu_tanh(y):
    return 0.5 * y * (1.0 + jnp.tanh(math.sqrt(2.0 / math.pi) * (y + 0.044715 * (y * y * y))))


def _s5post_block(y_ref, z_ref, wglu_ref, bglu_ref, o_ref):
    g = _gelu_tanh(y_ref[...].astype(F32))
    gate = jax.nn.sigmoid(_dot(g.astype(BF16), wglu_ref[...]) + bglu_ref[...])
    o_ref[...] = (g * gate * jax.nn.silu(z_ref[...].astype(F32))).astype(o_ref.dtype)


def _ret_tables(seq):
    t = RET_T
    inv = ROPE_BASE ** (-jnp.arange(0, RET_QK_HEAD, 2, dtype=F32) / RET_QK_HEAD)
    tile4 = lambda a: jnp.concatenate([a, a, a, a], axis=-1)
    ang_r = jnp.arange(t, dtype=F32)[:, None] * inv[None, :]
    ang_b = (jnp.arange(seq // t, dtype=F32) * float(t))[:, None] * inv[None, :]
    cos_r, sin_r = tile4(jnp.cos(ang_r)), tile4(jnp.sin(ang_r))
    cos_b, sin_b = tile4(jnp.cos(ang_b))[:, None, :], tile4(jnp.sin(ang_b))[:, None, :]
    half = RET_QK_HEAD // 2
    sign = jnp.asarray(np.tile(np.repeat([-1.0, 1.0], half), 2)[None, :], F32)

    log_g = np.log1p(-np.exp2(-5.0 - np.arange(RET_HEADS, dtype=np.float64)))
    i = np.arange(t)
    ci = i // CHUNK
    diff = (i[:, None] - i[None, :]).astype(np.float64)
    same = ci[:, None] == ci[None, :]
    earlier = ci[None, :] < ci[:, None]
    expo = np.where(same, np.abs(diff), diff)
    dmask = np.where((same | earlier)[None], np.exp(expo[None] * log_g[:, None, None]), 0.0)
    xi = np.exp((i + 1.0)[None, :] * log_g[:, None])
    zeta = np.exp((t - 1.0 - i)[None, :] * log_g[:, None])
    xi = np.broadcast_to(xi[:, :, None], (RET_HEADS, t, RET_V_HEAD))
    zeta = np.repeat(zeta.T, RET_QK_HEAD, axis=1)
    block_decay = [float(v) for v in np.exp(t * log_g)]
    return (cos_r, sin_r, cos_b, sin_b, sign, jnp.asarray(dmask, F32), jnp.asarray(xi, F32),
            jnp.asarray(zeta, F32), block_decay)


def _swap_halves(x):
    n = x.shape[1]
    half = RET_QK_HEAD // 2
    fwd = pltpu.roll(x, half, 1)
    bwd = pltpu.roll(x, n - half, 1)
    lane = lax.broadcasted_iota(jnp.int32, x.shape, 1)
    return jnp.where((lane & (RET_QK_HEAD - 1)) < half, bwd, fwd)


def _retention_block(block_decay, q_ref, k_ref, v_ref, z_ref, cosr_ref, sinr_ref, cosb_ref, sinb_ref,
                     sign_ref, dm_ref, xi_ref, zeta_ref, o_ref, state_ref):
    cb = cosb_ref[...]
    sb = sinb_ref[...]
    cos2 = cb * cosr_ref[...] - sb * sinr_ref[...]
    sin2 = (sb * cosr_ref[...] + cb * sinr_ref[...]) * sign_ref[...]
    pair_w = cos2.shape[1]
    per_pair = pair_w // RET_QK_HEAD
    for h in range(RET_HEADS):
        if h % per_pair == 0:
            cols = slice((h // per_pair) * pair_w, (h // per_pair + 1) * pair_w)
            q = q_ref[:, cols].astype(F32)
            k = k_ref[:, cols].astype(F32)
            qr = (q * cos2 + _swap_halves(q) * sin2).astype(BF16)
            ks = (k * cos2 + _swap_halves(k) * sin2) * (RET_QK_HEAD ** -0.5)
            kr = ks.astype(BF16)
            kz = (ks * zeta_ref[:, cols]).astype(BF16)
        qs = slice((h % per_pair) * RET_QK_HEAD, (h % per_pair + 1) * RET_QK_HEAD)
        vs = slice(h * RET_V_HEAD, (h + 1) * RET_V_HEAD)
        qh = qr[:, qs]
        kh = kr[:, qs]
        vh = v_ref[:, vs]
        s = lax.dot_general(qh, kh, (((1,), (1,)), ((), ())), preferred_element_type=F32)
        s = s * dm_ref[h]
        inner = _dot(s.astype(BF16), vh)
        st = state_ref[h]
        cross = _dot(qh, st.astype(BF16)) * xi_ref[h]
        kv = lax.dot_general(kz[:, qs], vh, (((0,), (0,)), ((), ())), preferred_element_type=F32)
        state_ref[h] = st * block_decay[h] + kv
        o = inner + cross
        o = o * lax.rsqrt(jnp.mean(o * o, axis=-1, keepdims=True) + EPS)
        o_ref[:, vs] = (o * jax.nn.silu(z_ref[:, vs].astype(F32))).astype(o_ref.dtype)


def _merge_block(ys_ref, yr_ref, gs0_ref, gs1_ref, gr0_ref, gr1_ref, x_ref, ws_ref, wr_ref, wo_ref,
                 g_ref, o_ref):
    a = _dot(ys_ref[...], ws_ref[...])
    b = _dot(yr_ref[...], wr_ref[...])
    gs = jnp.concatenate([gs0_ref[...], gs1_ref[...]], axis=1).astype(F32)
    gr = jnp.concatenate([gr0_ref[...], gr1_ref[...]], axis=1).astype(F32)
    m = jax.nn.sigmoid(gs) * a + jax.nn.sigmoid(gr) * b
    o = _dot(m.astype(BF16), wo_ref[...])
    o_ref[...] = x_ref[...] + _rms_rows(o, g_ref[...])


def _tail_kernel(block_decay,
                 yssm_ref, zs_ref, q_ref, k_ref, v_ref, zr_ref, cosb_ref, sinb_ref,
                 gs0_ref, gs1_ref, gr0_ref, gr1_ref, x_ref,
                 wglu_ref, bglu_ref, cosr_ref, sinr_ref, sign_ref, dm_ref, xi_ref, zeta_ref,
                 ws_ref, wr_ref, wo_ref, g_ref,
                 o_ref, state_ref, ys_scr, yr_scr):
    @pl.when(pl.program_id(0) == 0)
    def _():
        state_ref[...] = jnp.zeros_like(state_ref)
        ys_scr[...] = jnp.zeros_like(ys_scr)
        yr_scr[...] = jnp.zeros_like(yr_scr)

    _merge_block(ys_scr, yr_scr, gs0_ref, gs1_ref, gr0_ref, gr1_ref, x_ref, ws_ref, wr_ref, wo_ref,
                 g_ref, o_ref)
    _s5post_block(yssm_ref, zs_ref, wglu_ref, bglu_ref, ys_scr)
    _retention_block(block_decay, q_ref, k_ref, v_ref, zr_ref, cosr_ref, sinr_ref, cosb_ref, sinb_ref,
                     sign_ref, dm_ref, xi_ref, zeta_ref, yr_scr, state_ref)


def _tail(y_ssm, proj, x2, w_glu, b_glu, w_s, w_r, w_o, gain):
    seq, d = x2.shape
    t = RET_T
    nblk = seq // t
    cos_r, sin_r, cos_b, sin_b, sign, dmask, xi, zeta, block_decay = _ret_tables(seq)
    blk = 1024
    cur = lambda s: jnp.minimum(s, nblk - 1)
    prv = lambda s: jnp.maximum(s - 1, 0)
    single = pl.Buffered(1)
    const = lambda a: pl.BlockSpec(a.shape, lambda s: (0,) * a.ndim, pipeline_mode=single)
    b_glu2 = b_glu.reshape(1, S5_WIDTH).astype(F32)
    gain2 = gain.reshape(1, d).astype(F32)
    return pl.pallas_call(
        functools.partial(_tail_kernel, block_decay),
        grid=(nblk + 1,),
        in_specs=[
            pl.BlockSpec((t, S5_WIDTH), lambda s: (cur(s), 0)),
            pl.BlockSpec((t, S5_WIDTH), lambda s: (cur(s), _ZS_BLK)),
            pl.BlockSpec((t, RET_QK), lambda s: (cur(s), _Q_BLK512)),
            pl.BlockSpec((t, RET_QK), lambda s: (cur(s), _K_BLK512)),
            pl.BlockSpec((t, RET_WIDTH), lambda s: (cur(s), _V_BLK)),
            pl.BlockSpec((t, RET_WIDTH), lambda s: (cur(s), _ZR_BLK)),
            pl.BlockSpec((None, 1, cos_b.shape[2]), lambda s: (cur(s), 0, 0)),
            pl.BlockSpec((None, 1, sin_b.shape[2]), lambda s: (cur(s), 0, 0)),
            pl.BlockSpec((t, blk), lambda s: (prv(s), _GLS_BLK)),
            pl.BlockSpec((t, blk), lambda s: (prv(s), _GLS_BLK + 1)),
            pl.BlockSpec((t, blk), lambda s: (prv(s), _GLR_BLK)),
            pl.BlockSpec((t, blk), lambda s: (prv(s), _GLR_BLK + 1)),
            pl.BlockSpec((t, d), lambda s: (prv(s), 0)),
            const(w_glu), const(b_glu2), const(cos_r), const(sin_r), const(sign), const(dmask),
            const(xi), const(zeta), const(w_s), const(w_r), const(w_o), const(gain2),
        ],
        out_specs=pl.BlockSpec((t, d), lambda s: (prv(s), 0)),
        out_shape=jax.ShapeDtypeStruct((seq, d), F32),
        scratch_shapes=[
            pltpu.VMEM((RET_HEADS, RET_QK_HEAD, RET_V_HEAD), F32),
            pltpu.VMEM((t, S5_WIDTH), BF16),
            pltpu.VMEM((t, RET_WIDTH), BF16),
        ],
        compiler_params=pltpu.CompilerParams(
            dimension_semantics=("arbitrary",), vmem_limit_bytes=VMEM_LIMIT),
        name="tail",
    )(y_ssm, proj, proj, proj, proj, proj, cos_b, sin_b, proj, proj, proj, proj, x2,
      w_glu, b_glu2, cos_r, sin_r, sign, dmask, xi, zeta, w_s, w_r, w_o, gain2)


def kernel(x, norm_pre, w_in, lam_re, lam_im, log_step, s5_b_re, s5_b_im, s5_c_re, s5_c_im, s5_d,
           w_glu, b_glu, w_proj_s5, w_proj_ret, w_out, norm_post):
    bsz, seq, d = x.shape
    assert bsz == 1 and d == D_MODEL and seq % 1024 == 0
    depth = w_in.shape[0]
    x2 = x.reshape(seq, d)
    for l in range(depth):
        proj, (w_glu_b, w_s_b, w_r_b, w_o_b) = _inproj(
            x2, norm_pre[l], w_in[l].astype(BF16), (w_glu[l], w_proj_s5[l], w_proj_ret[l], w_out[l]))
        tables = _s5_tables(lam_re[l], lam_im[l], log_step[l], s5_b_re[l], s5_b_im[l],
                            s5_c_re[l], s5_c_im[l], s5_d[l], seq // S5_T)
        y_ssm = _s5core(proj, tables)
        x2 = _tail(y_ssm, proj, x2, w_glu_b, b_glu[l], w_s_b, w_r_b, w_o_b, norm_post[l])
    return x2.reshape(bsz, seq, d)
```

```python
import functools
import math

import numpy as np
import jax
import jax.numpy as jnp
from jax import lax
from jax.experimental import pallas as pl
from jax.experimental.pallas import tpu as pltpu

F32 = jnp.float32
BF16 = jnp.bfloat16
HIGHEST = lax.Precision.HIGHEST

D_MODEL = 2048
EPS = 1e-6
CHUNK = 64

S5_WIDTH = 1024
S5_GROUP = 16
S5_GROUPS = 64
S5_STATE = 64

RET_HEADS = 8
RET_V_HEAD = 128
RET_QK_HEAD = 64
RET_QK = 512
RET_WIDTH = 1024
ROPE_BASE = 10000.0

_Q_BLK512 = (2 * S5_WIDTH) // 512
_K_BLK512 = (2 * S5_WIDTH + RET_QK) // 512
_ZS_BLK = 1
_V_BLK = (2 * S5_WIDTH + 2 * RET_QK) // 1024
_ZR_BLK = _V_BLK + 1
_GLS_BLK = _ZR_BLK + 1
_GLR_BLK = _GLS_BLK + 2

VMEM_LIMIT = 56 * 1024 * 1024
INPROJ_VMEM_LIMIT = 60 * 1024 * 1024
INPROJ_PIECE = 1280
INPROJ_ROWS = 128
SUBLANES = 8
LANES = 128

S5_T = 32
S5_GB = 8
S5_TW = S5_T * S5_GROUP
S5_RB = 4096
S5_SCAN_ROWS = 16
S5_BATCH = 4

RET_T = 256


def _dot(a, b):
    return jnp.dot(a, b, preferred_element_type=F32)


def _rms_rows(x, gain):
    ms = jnp.mean(x * x, axis=-1, keepdims=True)
    return x * lax.rsqrt(ms + EPS) * gain


def _inproj_kernel(n_extra, x_hbm, g_ref, w_ref, *rest):
    extra_in = rest[:n_extra]
    o_ref = rest[n_extra]
    extra_out = rest[n_extra + 1:2 * n_extra + 1]
    h_ref, x_buf, x_sem = rest[2 * n_extra + 1:2 * n_extra + 4]

    tm, tn = o_ref.shape
    pieces = [(lo, min(lo + INPROJ_PIECE, tn)) for lo in range(0, tn, INPROJ_PIECE)]
    i = pl.program_id(0)

    def x_copy(blk, slot):
        rows = pl.ds(pl.multiple_of(blk * tm, tm), tm)
        return pltpu.make_async_copy(x_hbm.at[rows, :], x_buf.at[slot], x_sem.at[slot])

    @pl.when(pl.program_id(1) == 0)
    def _():
        slot = lax.rem(i, 2)

        @pl.when(i == 0)
        def _():
            x_copy(0, 0).start()

        @pl.when(i + 1 < pl.num_programs(0))
        def _():
            x_copy(i + 1, 1 - slot).start()

        x_copy(i, slot).wait()
        for r0 in range(0, tm, INPROJ_ROWS):
            rows = slice(r0, r0 + INPROJ_ROWS)
            hc = _rms_rows(x_buf[slot, rows, :], g_ref[...]).astype(BF16)
            h_ref[rows, :] = hc
            for lo, hi in pieces:
                o_ref[rows, lo:hi] = _dot(hc, w_ref[:, lo:hi]).astype(o_ref.dtype)

    @pl.when(pl.program_id(1) != 0)
    def _():
        for lo, hi in pieces:
            o_ref[:, lo:hi] = _dot(h_ref[...], w_ref[:, lo:hi]).astype(o_ref.dtype)

    for src, dst in zip(extra_in, extra_out):
        dst[...] = src[...].astype(dst.dtype)


def _inproj(x2, gain, w_bf16, later_weights, tm=1024, tn=2304):
    seq, d = x2.shape
    n = w_bf16.shape[1]
    grid = (seq // tm, n // tn)
    steps = grid[0] * grid[1]
    step_block = lambda i, j: (i * grid[1] + j, 0)
    extra_specs = [pl.BlockSpec((w.shape[0] // steps, w.shape[1]), step_block) for w in later_weights]
    outs = pl.pallas_call(
        functools.partial(_inproj_kernel, len(later_weights)),
        grid=grid,
        in_specs=[
            pl.BlockSpec(memory_space=pl.ANY),
            pl.BlockSpec((1, d), lambda i, j: (0, 0)),
            pl.BlockSpec((d, tn), lambda i, j: (0, j)),
        ] + extra_specs,
        out_specs=[pl.BlockSpec((tm, tn), lambda i, j: (i, j))] + extra_specs,
        out_shape=[jax.ShapeDtypeStruct((seq, n), BF16)]
        + [jax.ShapeDtypeStruct(w.shape, BF16) for w in later_weights],
        scratch_shapes=[pltpu.VMEM((tm, d), BF16), pltpu.VMEM((2, tm, d), F32),
                        pltpu.SemaphoreType.DMA((2,))],
        compiler_params=pltpu.CompilerParams(
            dimension_semantics=("arbitrary", "arbitrary"), vmem_limit_bytes=INPROJ_VMEM_LIMIT),
        name="inproj",
    )(x2, gain.reshape(1, d).astype(F32), w_bf16, *later_weights)
    return outs[0], outs[1:]


def _s5_tables(lam_re, lam_im, log_step, b_re, b_im, c_re, c_im, d_skip, n_chunks):
    t = S5_T
    lr = lam_re.astype(F32)
    li = lam_im.astype(F32)
    step = jnp.exp(log_step.astype(F32))[:, None]
    lrs = lr * step
    ang = li * step
    mag = jnp.exp(lrs)
    ab_re = mag * jnp.cos(ang)
    ab_im = mag * jnp.sin(ang)
    den = lr * lr + li * li
    nr = ab_re - 1.0
    f_re = (nr * lr + ab_im * li) / den
    f_im = (ab_im * lr - nr * li) / den
    br = b_re.astype(F32)
    bi = b_im.astype(F32)
    bb_re = f_re[..., None] * br - f_im[..., None] * bi
    bb_im = f_re[..., None] * bi + f_im[..., None] * br
    cr = c_re.astype(F32)
    ci = c_im.astype(F32)

    n_bits = int(math.log2(t))
    bit_re, bit_im = [ab_re], [ab_im]
    for _ in range(n_bits - 1):
        r, i = bit_re[-1], bit_im[-1]
        bit_re.append(r * r - i * i)
        bit_im.append(2.0 * r * i)
    pad_bits = ((0, 0), (0, 0), (0, SUBLANES - n_bits))
    abit_re = jnp.pad(jnp.stack(bit_re, axis=-1), pad_bits)
    abit_im = jnp.pad(jnp.stack(bit_im, axis=-1), pad_bits)
    bbt_re = jnp.tile(bb_re, (1, 1, LANES // S5_GROUP))
    bbt_im = jnp.tile(bb_im, (1, 1, LANES // S5_GROUP))
    e_ip = (jnp.arange(t, dtype=F32) + 1.0)[None, :, None]
    m_ip = jnp.exp(e_ip * lrs[:, None, :])
    pr = m_ip * jnp.cos(e_ip * ang[:, None, :])
    pi = m_ip * jnp.sin(e_ip * ang[:, None, :])
    p1 = jnp.concatenate([pr, -pi], axis=2)
    p2 = jnp.concatenate([-pi, -pr], axis=2)
    c_rr = jnp.concatenate([cr, cr], axis=2)
    c_ii = jnp.concatenate([ci, ci], axis=2)

    c_cat = jnp.concatenate([cr, -ci], axis=2)
    skip = jnp.broadcast_to(d_skip.astype(F32).reshape(S5_GROUPS, S5_GROUP, 1),
                            (S5_GROUPS, S5_GROUP, LANES))

    n_steps = max(1, int(math.ceil(math.log2(n_chunks))))
    m_t = jnp.exp(float(t) * lrs)
    sq_re, sq_im = [m_t * jnp.cos(float(t) * ang)], [m_t * jnp.sin(float(t) * ang)]
    for _ in range(n_steps - 1):
        r, i = sq_re[-1], sq_im[-1]
        sq_re.append(r * r - i * i)
        sq_im.append(2.0 * r * i)
    def pair_rows(sq):
        a = jnp.stack(sq, axis=0).reshape(n_steps, S5_GROUPS // 2, 2 * S5_STATE)
        a = jnp.transpose(a, (1, 0, 2))
        return jnp.pad(a, ((0, 0), (0, S5_SCAN_ROWS - n_steps), (0, 0)))

    per_group = (abit_re, abit_im, bbt_re, bbt_im, p1, p2, c_rr, c_ii, c_cat, skip)
    return per_group, pair_rows(sq_re), pair_rows(sq_im), n_steps


def _sublane_transpose(v):
    v = list(v)
    sub = lax.broadcasted_iota(jnp.int32, v[0].shape, 1)
    for k in (4, 2, 1):
        keep = (sub & k) == 0
        for r in range(SUBLANES):
            if r & k:
                continue
            a, b = v[r], v[r | k]
            v[r] = jnp.where(keep, a, pltpu.roll(b, k, 1))
            v[r | k] = jnp.where(keep, pltpu.roll(a, SUBLANES - k, 1), b)
    return v


def _s5core_kernel(n_steps, u_ref, abr_ref, abi_ref, bbr_ref, bbi_ref, p1_ref, p2_ref, crr_ref, cii_ref,
                   ccat_ref, skip_ref, apr_ref, api_ref, o_ref, rhs_scr, mt_scr, yt_scr):
    t = S5_T
    nch = rhs_scr.shape[2]
    nq = t // SUBLANES
    rb = min(S5_RB, nch * t)
    nb = rb // t
    nsub = nb // SUBLANES
    n_blocks = (nch * t) // rb

    for blk in range(n_blocks):
        x = u_ref[blk * rb:(blk + 1) * rb, :].astype(F32).reshape(nsub, SUBLANES, nq, SUBLANES, LANES)
        cols = slice(blk * nb, (blk + 1) * nb)
        for q in range(nq):
            w = _sublane_transpose([x[:, r, q] for r in range(SUBLANES)])
            for s in range(SUBLANES):
                i = q * SUBLANES + s
                rhs_scr[:, i * S5_GROUP:(i + 1) * S5_GROUP, cols] = (
                    w[s].reshape(nb, LANES).T.astype(BF16).reshape(S5_GB, S5_GROUP, nb))

    tail_lane = lax.broadcasted_iota(jnp.int32, (S5_GROUP, LANES), 1)
    tail_row = lax.broadcasted_iota(jnp.int32, (S5_GROUP, LANES), 0)
    lag0 = tail_lane == tail_row + (LANES - S5_GROUP)
    zero_half = jnp.zeros((S5_GROUP, S5_TW), F32)
    per_tile = LANES // S5_GROUP
    chunk_row = lax.broadcasted_iota(jnp.int32, (nch, LANES), 0)

    def shift_down(v, k):
        if k % SUBLANES == 0:
            return jnp.concatenate([jnp.zeros((k, LANES), F32), v[:nch - k]], axis=0)
        return jnp.where(chunk_row >= k, pltpu.roll(v, k, 0), 0.0)

    lane_expo = (t - 1) - lax.shift_right_logical(
        lax.broadcasted_iota(jnp.int32, (S5_STATE, S5_TW), 1), int(math.log2(S5_GROUP)))
    lane_reps = S5_TW // LANES

    def state_weights(g):
        abr = abr_ref[g]
        abi = abi_ref[g]
        pw_re = pw_im = None
        for b in range(int(math.log2(t))):
            bit = (lax.shift_right_logical(lane_expo, b) & 1) == 1
            f_re = jnp.where(bit, abr[:, b:b + 1], 1.0)
            f_im = jnp.where(bit, abi[:, b:b + 1], 0.0)
            if pw_re is None:
                pw_re, pw_im = f_re, f_im
            else:
                pw_re, pw_im = pw_re * f_re - pw_im * f_im, pw_re * f_im + pw_im * f_re
        br = jnp.concatenate([bbr_ref[g]] * lane_reps, axis=1)
        bi = jnp.concatenate([bbi_ref[g]] * lane_reps, axis=1)
        return jnp.concatenate([pw_re * br - pw_im * bi, pw_re * bi + pw_im * br], axis=0)

    def output_weights(g):
        crr = crr_ref[g]
        cii = cii_ref[g]
        p1 = p1_ref[g]
        p2 = p2_ref[g]
        return jnp.concatenate(
            [(crr * p1[i:i + 1, :] + cii * p2[i:i + 1, :]).astype(BF16) for i in range(t)], axis=0)

    def build_weights(g, slot):
        mt = mt_scr.at[slot]
        ws = state_weights(g)
        strip = jnp.dot(ccat_ref[g], ws, precision=HIGHEST, preferred_element_type=F32)
        tail = strip[:, S5_TW - LANES:] + jnp.where(lag0, skip_ref[g], 0.0)
        full = jnp.concatenate([strip[:, :S5_TW - LANES], tail, zero_half], axis=1)
        rot = [full] + [pltpu.roll(full, 2 * S5_TW - S5_GROUP * r, 1) for r in range(1, per_tile)]
        for i in range(t):
            a, r = divmod(t - 1 - i, per_tile)
            mt[i * S5_GROUP:(i + 1) * S5_GROUP, :] = (
                rot[r][:, a * LANES:a * LANES + S5_TW].astype(BF16))
        return ws.astype(BF16)

    def local_products(g, slot, ws_b):
        rhs = rhs_scr[g]
        yt_scr[g] = _dot(mt_scr[slot], rhs)
        return _dot(ws_b, rhs)

    def scan_pair(gp, g0, s0, s1):
        xr = jnp.concatenate([s0[:S5_STATE], s1[:S5_STATE]], axis=0).T
        xi = jnp.concatenate([s0[S5_STATE:], s1[S5_STATE:]], axis=0).T
        apr = apr_ref[gp]
        api = api_ref[gp]
        for k in range(n_steps):
            ar = apr[k:k + 1, :]
            ai = api[k:k + 1, :]
            sr = shift_down(xr, 1 << k)
            si = shift_down(xi, 1 << k)
            xr, xi = xr + (ar * sr - ai * si), xi + (ar * si + ai * sr)
        pr = shift_down(xr, 1).T
        pi = shift_down(xi, 1).T
        for j in range(2):
            rows = slice(j * S5_STATE, (j + 1) * S5_STATE)
            xprev = jnp.concatenate([pr[rows], pi[rows]], axis=0).astype(BF16)
            yt_scr[g0 + j] += _dot(output_weights(g0 + j), xprev)

    def per_batch(b, carry):
        g0 = S5_BATCH * b
        ws_b = [build_weights(g0, 0)]
        s = []
        for j in range(S5_BATCH):
            if j + 1 < S5_BATCH:
                ws_b.append(build_weights(g0 + j + 1, j + 1))
            s.append(local_products(g0 + j, j, ws_b[j]))
        for p in range(S5_BATCH // 2):
            scan_pair((S5_BATCH // 2) * b + p, g0 + 2 * p, s[2 * p], s[2 * p + 1])
        return carry

    lax.fori_loop(0, S5_GB // S5_BATCH, per_batch, 0)

    for blk in range(n_blocks):
        cols = slice(blk * nb, (blk + 1) * nb)
        per_q = []
        for q in range(nq):
            w = []
            for s in range(SUBLANES):
                i = q * SUBLANES + s
                piece = yt_scr[:, i * S5_GROUP:(i + 1) * S5_GROUP, cols].reshape(S5_GB * S5_GROUP, nb)
                w.append(piece.T.reshape(nsub, SUBLANES, LANES))
            per_q.append(_sublane_transpose(w))
        x = jnp.stack([jnp.stack([per_q[q][r] for q in range(nq)], axis=1)
                       for r in range(SUBLANES)], axis=1)
        o_ref[blk * rb:(blk + 1) * rb, :] = x.reshape(rb, LANES).astype(o_ref.dtype)


def _s5core(proj, tables):
    seq = proj.shape[0]
    per_group, ap_re, ap_im, n_steps = tables
    nch = seq // S5_T
    lanes = S5_GB * S5_GROUP
    grp = lambda i: (i, 0, 0)
    gspec = lambda a: pl.BlockSpec((S5_GB,) + a.shape[1:], grp)
    pspec = lambda a: pl.BlockSpec((S5_GB // 2,) + a.shape[1:], grp)
    return pl.pallas_call(
        functools.partial(_s5core_kernel, n_steps),
        grid=(S5_GROUPS // S5_GB,),
        in_specs=([pl.BlockSpec((seq, lanes), lambda i: (0, i))] + [gspec(a) for a in per_group]
                  + [pspec(ap_re), pspec(ap_im)]),
        out_specs=pl.BlockSpec((seq, lanes), lambda i: (0, i)),
        out_shape=jax.ShapeDtypeStruct((seq, S5_WIDTH), BF16),
        scratch_shapes=[
            pltpu.VMEM((S5_GB, S5_TW, nch), BF16),
            pltpu.VMEM((S5_BATCH, S5_TW, S5_TW), BF16),
            pltpu.VMEM((S5_GB, S5_TW, nch), F32),
        ],
        compiler_params=pltpu.CompilerParams(
            dimension_semantics=("parallel",), vmem_limit_bytes=VMEM_LIMIT),
        name="s5_core",
    )(proj, *per_group, ap_re, ap_im)


def _gelu_tanh(y):
    return 0.5 * y * (1.0 + jnp.tanh(math.sqrt(2.0 / math.pi) * (y + 0.044715 * (y * y * y))))


def _s5post_block(y_ref, z_ref, wglu_ref, bglu_ref, o_ref):
    g = _gelu_tanh(y_ref[...].astype(F32))
    gate = jax.nn.sigmoid(_dot(g.astype(BF16), wglu_ref[...]) + bglu_ref[...])
    o_ref[...] = (g * gate * jax.nn.silu(z_ref[...].astype(F32))).astype(o_ref.dtype)


def _ret_tables(seq):
    t = RET_T
    inv = ROPE_BASE ** (-jnp.arange(0, RET_QK_HEAD, 2, dtype=F32) / RET_QK_HEAD)
    tile4 = lambda a: jnp.concatenate([a, a, a, a], axis=-1)
    ang_r = jnp.arange(t, dtype=F32)[:, None] * inv[None, :]
    ang_b = (jnp.arange(seq // t, dtype=F32) * float(t))[:, None] * inv[None, :]
    cos_r, sin_r = tile4(jnp.cos(ang_r)), tile4(jnp.sin(ang_r))
    cos_b, sin_b = tile4(jnp.cos(ang_b))[:, None, :], tile4(jnp.sin(ang_b))[:, None, :]
    half = RET_QK_HEAD // 2
    sign = jnp.asarray(np.tile(np.repeat([-1.0, 1.0], half), 2)[None, :], F32)

    log_g = np.log1p(-np.exp2(-5.0 - np.arange(RET_HEADS, dtype=np.float64)))
    i = np.arange(t)
    ci = i // CHUNK
    diff = (i[:, None] - i[None, :]).astype(np.float64)
    same = ci[:, None] == ci[None, :]
    earlier = ci[None, :] < ci[:, None]
    expo = np.where(same, np.abs(diff), diff)
    dmask = np.where((same | earlier)[None], np.exp(expo[None] * log_g[:, None, None]), 0.0)
    xi = np.exp((i + 1.0)[None, :] * log_g[:, None])
    zeta = np.exp((t - 1.0 - i)[None, :] * log_g[:, None])
    xi = np.broadcast_to(xi[:, :, None], (RET_HEADS, t, RET_V_HEAD))
    zeta = np.repeat(zeta.T, RET_QK_HEAD, axis=1)
    block_decay = [float(v) for v in np.exp(t * log_g)]
    return (cos_r, sin_r, cos_b, sin_b, sign, jnp.asarray(dmask, F32), jnp.asarray(xi, F32),
            jnp.asarray(zeta, F32), block_decay)


def _swap_halves(x):
    n = x.shape[1]
    half = RET_QK_HEAD // 2
    fwd = pltpu.roll(x, half, 1)
    bwd = pltpu.roll(x, n - half, 1)
    lane = lax.broadcasted_iota(jnp.int32, x.shape, 1)
    return jnp.where((lane & (RET_QK_HEAD - 1)) < half, bwd, fwd)


def _retention_block(block_decay, q_ref, k_ref, v_ref, z_ref, cosr_ref, sinr_ref, cosb_ref, sinb_ref,
                     sign_ref, dm_ref, xi_ref, zeta_ref, o_ref, state_ref):
    cb = cosb_ref[...]
    sb = sinb_ref[...]
    cos2 = cb * cosr_ref[...] - sb * sinr_ref[...]
    sin2 = (sb * cosr_ref[...] + cb * sinr_ref[...]) * sign_ref[...]
    pair_w = cos2.shape[1]
    per_pair = pair_w // RET_QK_HEAD
    for h in range(RET_HEADS):
        if h % per_pair == 0:
            cols = slice((h // per_pair) * pair_w, (h // per_pair + 1) * pair_w)
            q = q_ref[:, cols].astype(F32)
            k = k_ref[:, cols].astype(F32)
            qr = (q * cos2 + _swap_halves(q) * sin2).astype(BF16)
            ks = (k * cos2 + _swap_halves(k) * sin2) * (RET_QK_HEAD ** -0.5)
            kr = ks.astype(BF16)
            kz = (ks * zeta_ref[:, cols]).astype(BF16)
        qs = slice((h % per_pair) * RET_QK_HEAD, (h % per_pair + 1) * RET_QK_HEAD)
        vs = slice(h * RET_V_HEAD, (h + 1) * RET_V_HEAD)
        qh = qr[:, qs]
        kh = kr[:, qs]
        vh = v_ref[:, vs]
        s = lax.dot_general(qh, kh, (((1,), (1,)), ((), ())), preferred_element_type=F32)
        s = s * dm_ref[h]
        inner = _dot(s.astype(BF16), vh)
        st = state_ref[h]
        cross = _dot(qh, st.astype(BF16)) * xi_ref[h]
        kv = lax.dot_general(kz[:, qs], vh, (((0,), (0,)), ((), ())), preferred_element_type=F32)
        state_ref[h] = st * block_decay[h] + kv
        o = inner + cross
        o = o * lax.rsqrt(jnp.mean(o * o, axis=-1, keepdims=True) + EPS)
        o_ref[:, vs] = (o * jax.nn.silu(z_ref[:, vs].astype(F32))).astype(o_ref.dtype)


def _merge_block(ys_ref, yr_ref, gs0_ref, gs1_ref, gr0_ref, gr1_ref, x_ref, ws_ref, wr_ref, wo_ref,
                 g_ref, o_ref):
    a = _dot(ys_ref[...], ws_ref[...])
    b = _dot(yr_ref[...], wr_ref[...])
    gs = jnp.concatenate([gs0_ref[...], gs1_ref[...]], axis=1).astype(F32)
    gr = jnp.concatenate([gr0_ref[...], gr1_ref[...]], axis=1).astype(F32)
    m = jax.nn.sigmoid(gs) * a + jax.nn.sigmoid(gr) * b
    o = _dot(m.astype(BF16), wo_ref[...])
    o_ref[...] = x_ref[...] + _rms_rows(o, g_ref[...])


def _tail_kernel(block_decay,
                 yssm_ref, zs_ref, q_ref, k_ref, v_ref, zr_ref, cosb_ref, sinb_ref,
                 gs0_ref, gs1_ref, gr0_ref, gr1_ref, x_ref,
                 wglu_ref, bglu_ref, cosr_ref, sinr_ref, sign_ref, dm_ref, xi_ref, zeta_ref,
                 ws_ref, wr_ref, wo_ref, g_ref,
                 o_ref, state_ref, ys_scr, yr_scr):
    @pl.when(pl.program_id(0) == 0)
    def _():
        state_ref[...] = jnp.zeros_like(state_ref)
        ys_scr[...] = jnp.zeros_like(ys_scr)
        yr_scr[...] = jnp.zeros_like(yr_scr)

    _merge_block(ys_scr, yr_scr, gs0_ref, gs1_ref, gr0_ref, gr1_ref, x_ref, ws_ref, wr_ref, wo_ref,
                 g_ref, o_ref)
    _s5post_block(yssm_ref, zs_ref, wglu_ref, bglu_ref, ys_scr)
    _retention_block(block_decay, q_ref, k_ref, v_ref, zr_ref, cosr_ref, sinr_ref, cosb_ref, sinb_ref,
                     sign_ref, dm_ref, xi_ref, zeta_ref, yr_scr, state_ref)


def _tail(y_ssm, proj, x2, w_glu, b_glu, w_s, w_r, w_o, gain):
    seq, d = x2.shape
    t = RET_T
    nblk = seq // t
    cos_r, sin_r, cos_b, sin_b, sign, dmask, xi, zeta, block_decay = _ret_tables(seq)
    blk = 1024
    cur = lambda s: jnp.minimum(s, nblk - 1)
    prv = lambda s: jnp.maximum(s - 1, 0)
    single = pl.Buffered(1)
    const = lambda a: pl.BlockSpec(a.shape, lambda s: (0,) * a.ndim, pipeline_mode=single)
    b_glu2 = b_glu.reshape(1, S5_WIDTH).astype(F32)
    gain2 = gain.reshape(1, d).astype(F32)
    return pl.pallas_call(
        functools.partial(_tail_kernel, block_decay),
        grid=(nblk + 1,),
        in_specs=[
            pl.BlockSpec((t, S5_WIDTH), lambda s: (cur(s), 0)),
            pl.BlockSpec((t, S5_WIDTH), lambda s: (cur(s), _ZS_BLK)),
            pl.BlockSpec((t, RET_QK), lambda s: (cur(s), _Q_BLK512)),
            pl.BlockSpec((t, RET_QK), lambda s: (cur(s), _K_BLK512)),
            pl.BlockSpec((t, RET_WIDTH), lambda s: (cur(s), _V_BLK)),
            pl.BlockSpec((t, RET_WIDTH), lambda s: (cur(s), _ZR_BLK)),
            pl.BlockSpec((None, 1, cos_b.shape[2]), lambda s: (cur(s), 0, 0)),
            pl.BlockSpec((None, 1, sin_b.shape[2]), lambda s: (cur(s), 0, 0)),
            pl.BlockSpec((t, blk), lambda s: (prv(s), _GLS_BLK)),
            pl.BlockSpec((t, blk), lambda s: (prv(s), _GLS_BLK + 1)),
            pl.BlockSpec((t, blk), lambda s: (prv(s), _GLR_BLK)),
            pl.BlockSpec((t, blk), lambda s: (prv(s), _GLR_BLK + 1)),
            pl.BlockSpec((t, d), lambda s: (prv(s), 0)),
            const(w_glu), const(b_glu2), const(cos_r), const(sin_r), const(sign), const(dmask),
            const(xi), const(zeta), const(w_s), const(w_r), const(w_o), const(gain2),
        ],
        out_specs=pl.BlockSpec((t, d), lambda s: (prv(s), 0)),
        out_shape=jax.ShapeDtypeStruct((seq, d), F32),
        scratch_shapes=[
            pltpu.VMEM((RET_HEADS, RET_QK_HEAD, RET_V_HEAD), F32),
            pltpu.VMEM((t, S5_WIDTH), BF16),
            pltpu.VMEM((t, RET_WIDTH), BF16),
        ],
        compiler_params=pltpu.CompilerParams(
            dimension_semantics=("arbitrary",), vmem_limit_bytes=VMEM_LIMIT),
        name="tail",
    )(y_ssm, proj, proj, proj, proj, proj, cos_b, sin_b, proj, proj, proj, proj, x2,
      w_glu, b_glu2, cos_r, sin_r, sign, dmask, xi, zeta, w_s, w_r, w_o, gain2)


def kernel(x, norm_pre, w_in, lam_re, lam_im, log_step, s5_b_re, s5_b_im, s5_c_re, s5_c_im, s5_d,
           w_glu, b_glu, w_proj_s5, w_proj_ret, w_out, norm_post):
    bsz, seq, d = x.shape
    assert bsz == 1 and d == D_MODEL and seq % 1024 == 0
    depth = w_in.shape[0]
    x2 = x.reshape(seq, d)
    for l in range(depth):
        proj, (w_glu_b, w_s_b, w_r_b, w_o_b) = _inproj(
            x2, norm_pre[l], w_in[l].astype(BF16), (w_glu[l], w_proj_s5[l], w_proj_ret[l], w_out[l]))
        tables = _s5_tables(lam_re[l], lam_im[l], log_step[l], s5_b_re[l], s5_b_im[l],
                            s5_c_re[l], s5_c_im[l], s5_d[l], seq // S5_T)
        y_ssm = _s5core(proj, tables)
        x2 = _tail(y_ssm, proj, x2, w_glu_b, b_glu[l], w_s_b, w_r_b, w_o_b, norm_post[l])
    return x2.reshape(bsz, seq, d)
```
